```python
import math
import jax, jax.numpy as jnp
from jax import lax
import numpy as np

D_MODEL = 1024
BATCH = 1
SEQ = 16384
DEPTH = 4

HEAD_DIM = 64
A_HEADS = 4
B_HEADS = 4
C_HEADS = 8
C_KV_HEADS = 2
A_W = A_HEADS * HEAD_DIM
B_W = B_HEADS * HEAD_DIM
C_QW = C_HEADS * HEAD_DIM
C_KVW = C_KV_HEADS * HEAD_DIM
N_BRANCH = 3
IN_SIZES = (A_W, A_W, A_W, B_W, B_W, B_W, B_HEADS, C_QW, C_KVW, C_KVW, D_MODEL, D_MODEL, D_MODEL)
IN_COLS = sum(IN_SIZES)
MOBA_BLOCK = 256
MOBA_TOPK = 3
Q_BLOCK = 128
WINDOW = 128
NUM_BUCKETS = 32
MAX_DISTANCE = 4096
D_FF = 2816
CONV_WIDTH = 3
EPS = 1e-6
NEG = -1e30
SCALE = HEAD_DIM ** -0.5

kernel_name = "hybrid_moba_fox_swa_gated_block"


def rmsnorm(x, g):
    xf = x.astype(jnp.float32)
    y = xf * lax.rsqrt(jnp.mean(xf * xf, axis=-1, keepdims=True) + EPS)
    return (y * g.astype(jnp.float32)).astype(x.dtype)


def t5_bucket(dist):
    dist = jnp.maximum(dist, 0)
    max_exact = NUM_BUCKETS // 2
    d = jnp.maximum(dist.astype(jnp.float32), 1.0)
    large = max_exact + (jnp.log(d / max_exact) / math.log(MAX_DISTANCE / max_exact)
                         * (NUM_BUCKETS - max_exact)).astype(jnp.int32)
    large = jnp.minimum(large, NUM_BUCKETS - 1)
    return jnp.where(dist < max_exact, dist, large)


def moba_attention(q, k, v, tbl):
    Bn, S, H, Dh = q.shape
    s_pad = -(-S // MOBA_BLOCK) * MOBA_BLOCK
    nb = s_pad // MOBA_BLOCK
    ksel = min(MOBA_TOPK, nb)
    qt = q.transpose(0, 2, 1, 3)
    pad = ((0, 0), (0, 0), (0, s_pad - S), (0, 0))
    k_blocks = jnp.pad(k.transpose(0, 2, 1, 3), pad).reshape(Bn, H, nb, MOBA_BLOCK, Dh)
    v_blocks = jnp.pad(v.transpose(0, 2, 1, 3), pad).reshape(Bn, H, nb, MOBA_BLOCK, Dh)
    k_mean = jnp.mean(k_blocks.astype(jnp.float32), axis=3)
    gate = jnp.einsum('bhsd,bhnd->bhsn', qt.astype(jnp.float32), k_mean)
    q_blk = jnp.arange(S) // MOBA_BLOCK
    past = jnp.arange(nb)[None, :] < q_blk[:, None]
    gate = jnp.where(past, gate, -jnp.inf)
    _, sel = lax.top_k(gate, ksel)
    valid = sel < q_blk[:, None]
    nq = S // Q_BLOCK
    qc = qt.reshape(Bn, H, nq, Q_BLOCK, Dh).transpose(2, 0, 1, 3, 4)
    selc = sel.reshape(Bn, H, nq, Q_BLOCK, ksel).transpose(2, 0, 1, 3, 4)
    validc = valid.reshape(Bn, H, nq, Q_BLOCK, ksel).transpose(2, 0, 1, 3, 4)
    bi = jnp.arange(Bn)[:, None, None, None]
    hi = jnp.arange(H)[None, :, None, None]
    blk_ar = jnp.arange(MOBA_BLOCK)

    def chunk(args):
        c, qb, sb, vb = args
        t = c * Q_BLOCK + jnp.arange(Q_BLOCK)
        kg = k_blocks[bi, hi, sb].reshape(Bn, H, Q_BLOCK, ksel * MOBA_BLOCK, Dh)
        vg = v_blocks[bi, hi, sb].reshape(Bn, H, Q_BLOCK, ksel * MOBA_BLOCK, Dh)
        s_pos = (sb[..., None] * MOBA_BLOCK + blk_ar).reshape(Bn, H, Q_BLOCK, ksel * MOBA_BLOCK)
        lg = jnp.einsum('bhqd,bhqkd->bhqk', qb, kg).astype(jnp.float32) * SCALE
        lg = lg + tbl[hi, t5_bucket(t[:, None] - s_pos)].astype(jnp.float32)
        lg = jnp.where(jnp.repeat(vb, MOBA_BLOCK, axis=-1), lg, NEG)
        own = (c * Q_BLOCK) // MOBA_BLOCK
        ko = lax.dynamic_index_in_dim(k_blocks, own, axis=2, keepdims=False)
        vo = lax.dynamic_index_in_dim(v_blocks, own, axis=2, keepdims=False)
        dist = t[:, None] - (own * MOBA_BLOCK + blk_ar)[None, :]
        lo = jnp.einsum('bhqd,bhkd->bhqk', qb, ko).astype(jnp.float32) * SCALE
        lo = lo + tbl[:, t5_bucket(dist)].astype(jnp.float32)
        lo = jnp.where(dist >= 0, lo, NEG)
        p = jax.nn.softmax(jnp.concatenate([lg, lo], axis=-1), axis=-1)
        pg = p[..., :ksel * MOBA_BLOCK].astype(v.dtype)
        po = p[..., ksel * MOBA_BLOCK:].astype(v.dtype)
        return (jnp.einsum('bhqk,bhqkd->bhqd', pg, vg)
                + jnp.einsum('bhqk,bhkd->bhqd', po, vo))

    out = lax.map(chunk, (jnp.arange(nq), qc, selc, validc))
    return out.transpose(1, 0, 3, 2, 4).reshape(Bn, S, H * Dh)


def forgetting_attention(q, k, v, f_logit):
    Bn, S, H, Dh = q.shape
    logf = jax.nn.log_sigmoid(f_logit.astype(jnp.float32))
    cum = jnp.cumsum(logf, axis=1).transpose(0, 2, 1)
    kt = k.transpose(0, 2, 1, 3)
    vt = v.transpose(0, 2, 1, 3)
    nq = S // Q_BLOCK
    qc = q.transpose(0, 2, 1, 3).reshape(Bn, H, nq, Q_BLOCK, Dh).transpose(2, 0, 1, 3, 4)
    cc = cum.reshape(Bn, H, nq, Q_BLOCK).transpose(2, 0, 1, 3)
    s_pos = jnp.arange(S)

    def block(args):
        i, qb, cb = args
        t = i * Q_BLOCK + jnp.arange(Q_BLOCK)
        lg = jnp.einsum('bhqd,bhkd->bhqk', qb, kt).astype(jnp.float32) * SCALE
        lg = lg + cb[..., None] - cum[:, :, None, :]
        lg = jnp.where(s_pos[None, :] <= t[:, None], lg, NEG)
        p = jax.nn.softmax(lg, axis=-1).astype(vt.dtype)
        return jnp.einsum('bhqk,bhkd->bhqd', p, vt)

    out = lax.map(block, (jnp.arange(nq), qc, cc))
    return out.transpose(1, 0, 3, 2, 4).reshape(Bn, S, H * Dh)


def sliding_window_sink_attention(q, k, v, sinks, tbl):
    Bn, S, Hq, Dh = q.shape
    Hkv = k.shape[2]
    G = Hq // Hkv
    nq = S // Q_BLOCK
    qb = q.reshape(Bn, nq, Q_BLOCK, Hkv, G, Dh)
    kb = k.reshape(Bn, nq, Q_BLOCK, Hkv, Dh)
    vb = v.reshape(Bn, nq, Q_BLOCK, Hkv, Dh)
    shift = ((0, 0), (1, 0), (0, 0), (0, 0), (0, 0))
    kband = jnp.concatenate([jnp.pad(kb, shift)[:, :-1], kb], axis=2)
    vband = jnp.concatenate([jnp.pad(vb, shift)[:, :-1], vb], axis=2)
    lg = jnp.einsum('bnqhgd,bnkhd->bnhgqk', qb, kband).astype(jnp.float32) * SCALE
    tl = jnp.arange(Q_BLOCK)
    sl = jnp.arange(2 * Q_BLOCK)
    dist = tl[:, None] + Q_BLOCK - sl[None, :]
    key_pos = (jnp.arange(nq)[:, None] - 1) * Q_BLOCK + sl[None, :]
    mask = ((dist >= 0) & (dist < WINDOW))[None] & (key_pos >= 0)[:, None, :]
    bias = tbl[:, t5_bucket(dist)].astype(jnp.float32).reshape(Hkv, G, Q_BLOCK, 2 * Q_BLOCK)
    lg = jnp.where(mask[None, :, None, None], lg + bias, NEG)
    sink = jnp.broadcast_to(sinks.astype(jnp.float32).reshape(Hkv, G)[:, :, None, None],
                            lg.shape[:-1] + (1,))
    p = jax.nn.softmax(jnp.concatenate([lg, sink], axis=-1), axis=-1)[..., :-1]
    out = jnp.einsum('bnhgqk,bnkhd->bnqhgd', p.astype(v.dtype), vband)
    return out.reshape(Bn, S, Hq * Dh)


def causal_dwconv(u, w, b):
    C = u.shape[-1]
    y = lax.conv_general_dilated(u, w[:, None, :].astype(u.dtype), window_strides=(1,),
                                 padding=[(CONV_WIDTH - 1, 0)],
                                 dimension_numbers=('NWC', 'WIO', 'NWC'),
                                 feature_group_count=C)
    return y + b


def hybrid_layer(x, ln1, w_in, b_f, sinks, w_pa, w_pb, w_pc, w_o, ln2, w_up, conv_w, conv_b,
                 w_down, tbl_a, tbl_c):
    Bn, S, _ = x.shape
    h = rmsnorm(x, ln1)
    z = h @ w_in
    points = np.cumsum(IN_SIZES)[:-1].tolist()
    qa, ka, va, qb, kb, vb, fb, qc, kc, vc, ga, gb, gc = jnp.split(z, points, axis=-1)

    def heads(t, n):
        return t.reshape(Bn, S, n, HEAD_DIM)

    o_a = moba_attention(heads(qa, A_HEADS), heads(ka, A_HEADS), heads(va, A_HEADS), tbl_a)
    o_b = forgetting_attention(heads(qb, B_HEADS), heads(kb, B_HEADS), heads(vb, B_HEADS), fb + b_f)
    o_c = sliding_window_sink_attention(heads(qc, C_HEADS), heads(kc, C_KV_HEADS),
                                        heads(vc, C_KV_HEADS), sinks, tbl_c)
    merged = (jax.nn.sigmoid(ga) * (o_a @ w_pa)
              + jax.nn.sigmoid(gb) * (o_b @ w_pb)
              + jax.nn.sigmoid(gc) * (o_c @ w_pc))
    x = x + merged @ w_o

    h2 = rmsnorm(x, ln2)
    u = causal_dwconv(h2 @ w_up, conv_w, conv_b)
    a, b = jnp.split(u, 2, axis=-1)
    return x + (jax.nn.silu(a) * b) @ w_down


def setup_inputs(seed: int = 0) -> dict:
    key = jax.random.key(seed)
    ks = jax.random.split(key, 16)

    def nrm(k, shape, scale):
        return jax.random.normal(k, shape, jnp.float32) * scale

    return {
        "x": nrm(ks[0], (BATCH, SEQ, D_MODEL), 1.0),
        "ln1": 1.0 + nrm(ks[1], (DEPTH, D_MODEL), 0.05),
        "w_in": nrm(ks[2], (DEPTH, D_MODEL, IN_COLS), D_MODEL ** -0.5),
        "b_f": 4.0 + nrm(ks[3], (DEPTH, B_HEADS), 0.5),
        "sinks": nrm(ks[4], (DEPTH, C_HEADS), 1.0),
        "w_pa": nrm(ks[5], (DEPTH, A_W, D_MODEL), A_W ** -0.5),
        "w_pb": nrm(ks[6], (DEPTH, B_W, D_MODEL), B_W ** -0.5),
        "w_pc": nrm(ks[7], (DEPTH, C_QW, D_MODEL), C_QW ** -0.5),
        "w_o": nrm(ks[8], (DEPTH, D_MODEL, D_MODEL), D_MODEL ** -0.5),
        "ln2": 1.0 + nrm(ks[9], (DEPTH, D_MODEL), 0.05),
        "w_up": nrm(ks[10], (DEPTH, D_MODEL, 2 * D_FF), D_MODEL ** -0.5),
        "conv_w": nrm(ks[11], (DEPTH, CONV_WIDTH, 2 * D_FF), CONV_WIDTH ** -0.5),
        "conv_b": nrm(ks[12], (DEPTH, 2 * D_FF), 0.02),
        "w_down": nrm(ks[13], (DEPTH, D_FF, D_MODEL), D_FF ** -0.5),
        "rel_bias": nrm(ks[14], (NUM_BUCKETS, A_HEADS + C_HEADS), 0.5),
        "ln_f": 1.0 + nrm(ks[15], (D_MODEL,), 0.05),
    }


def reference(x, ln1, w_in, b_f, sinks, w_pa, w_pb, w_pc, w_o, ln2, w_up, conv_w, conv_b,
              w_down, rel_bias, ln_f):
    tbl_a = rel_bias[:, :A_HEADS].T
    tbl_c = rel_bias[:, A_HEADS:].T
    for l in range(DEPTH):
        x = hybrid_layer(x, ln1[l], w_in[l], b_f[l], sinks[l], w_pa[l], w_pb[l], w_pc[l], w_o[l],
                         ln2[l], w_up[l], conv_w[l], conv_b[l], w_down[l], tbl_a, tbl_c)
    return rmsnorm(x, ln_f)
```

```python
import functools
import math

import numpy as np
import jax
import jax.numpy as jnp
from jax import lax
from jax.experimental import pallas as pl
from jax.experimental.pallas import tpu as pltpu

D_MODEL = 1024
DEPTH = 4
HEAD_DIM = 64
N_HEADS = 4
C_HEADS = 8
C_GROUP = 4
MOBA_BLOCK = 256
MOBA_TOPK = 3
Q_BLOCK = 128
WINDOW = 128
NUM_BUCKETS = 32
MAX_DISTANCE = 4096
D_FF = 2816
EPS = 1e-6
NEG = -1e30
SCALE = HEAD_DIM ** -0.5

LANES = 128
HALO = 8
TM = 512
TA = 512
FF_CHUNK = 256
VMEM_LIMIT = 56 * 1024 * 1024

_A_QK = (0, 1024)
_A_V = (1024, 1536)
_B_QK = (1536, 2560)
_B_V = (2560, 3072)
_C_ALL = (3072, 4480)
_F_B = (4480, 4608)
_W1_COLS = 4608
_C_K = 1024
_C_V = 1152
_C_W = 1408

bf16 = jnp.bfloat16
f32 = jnp.float32
_HI = lax.Precision.HIGHEST


def _rms(x, g):
    ms = jnp.mean(x * x, axis=-1, keepdims=True)
    return x * lax.rsqrt(ms + EPS) * g


def _t5_bucket(dist):
    dist = jnp.maximum(dist, 0)
    max_exact = NUM_BUCKETS // 2
    d = jnp.maximum(dist.astype(jnp.float32), 1.0)
    large = max_exact + (jnp.log(d / max_exact) / math.log(MAX_DISTANCE / max_exact)
                         * (NUM_BUCKETS - max_exact)).astype(jnp.int32)
    large = jnp.minimum(large, NUM_BUCKETS - 1)
    return jnp.where(dist < max_exact, dist, large)


def _params(*sem):
    return pltpu.CompilerParams(dimension_semantics=sem, vmem_limit_bytes=VMEM_LIMIT)


def _resident(shape, index_map):
    return pl.BlockSpec(shape, index_map, pipeline_mode=pl.Buffered(1))


def _inproj_kernel(x_ref, g_ref, w_ref, aqk_ref, av_ref, bqk_ref, bv_ref, c_ref, fb_ref, km_ref):
    h = _rms(x_ref[...], g_ref[...]).astype(bf16)

    def mm(span):
        return jnp.dot(h, w_ref[:, span[0]:span[1]], preferred_element_type=f32)

    aqk = mm(_A_QK)
    aqk_ref[...] = aqk
    ka = aqk[:, 512:1024]
    km_ref[0] = jnp.mean(ka.reshape(TM // MOBA_BLOCK, MOBA_BLOCK, 512), axis=1)
    av_ref[...] = mm(_A_V).astype(bf16)
    bqk_ref[...] = mm(_B_QK)
    bv_ref[...] = mm(_B_V).astype(bf16)
    c_ref[...] = mm(_C_ALL).astype(bf16)
    fb_ref[...] = mm(_F_B)


def _inproj(x2, g, w1):
    s = x2.shape[0]
    nt = s // TM
    row = lambda w: pl.BlockSpec((TM, w), lambda i: (i, 0))
    return pl.pallas_call(
        _inproj_kernel,
        grid=(nt,),
        in_specs=[row(D_MODEL),
                  pl.BlockSpec((1, D_MODEL), lambda i: (0, 0)),
                  _resident((D_MODEL, _W1_COLS), lambda i: (0, 0))],
        out_specs=[row(1024), row(512), row(1024), row(512), row(_C_W), row(LANES),
                   pl.BlockSpec((1, TM // MOBA_BLOCK, 512), lambda i: (i, 0, 0))],
        out_shape=[jax.ShapeDtypeStruct((s, 1024), f32),
                   jax.ShapeDtypeStruct((s, 512), bf16),
                   jax.ShapeDtypeStruct((s, 1024), f32),
                   jax.ShapeDtypeStruct((s, 512), bf16),
                   jax.ShapeDtypeStruct((s, _C_W), bf16),
                   jax.ShapeDtypeStruct((s, LANES), f32),
                   jax.ShapeDtypeStruct((nt, TM // MOBA_BLOCK, 512), f32)],
        compiler_params=_params("arbitrary"),
        name="inproj",
    )(x2, g, w1)


def _moba_prep_kernel(aqk_ref, av_ref, km_ref, q_ref, k_ref, v_ref, *, nblk):
    i = pl.program_id(0)
    lane = lax.broadcasted_iota(jnp.int32, (TM, LANES), 1)
    kblk = (i * TM + lax.broadcasted_iota(jnp.int32, (TM, LANES), 0)) >> 8
    bidx = lax.broadcasted_iota(jnp.int32, (nblk, TM), 0)
    bidx_f = bidx.astype(f32)
    qblk = (i * TM + lax.broadcasted_iota(jnp.int32, (nblk, TM), 1)) >> 8
    for h in range(N_HEADS):
        q = aqk_ref[:, LANES * h:LANES * (h + 1)]
        km = km_ref[:, LANES * h:LANES * (h + 1)]
        gate = lax.dot_general(km, q, (((1,), (1,)), ((), ())), precision=_HI,
                               preferred_element_type=f32)
        gate = jnp.where(bidx < qblk, gate, -jnp.inf)
        sel = jnp.zeros((nblk, TM), f32)
        for _ in range(MOBA_TOPK):
            m = jnp.max(gate, axis=0, keepdims=True)
            first = jnp.min(jnp.where(gate == m, bidx_f, float(nblk)), axis=0, keepdims=True)
            hit = jnp.logical_and(bidx_f == first, m > -jnp.inf)
            sel = jnp.where(hit, 1.0, sel)
            gate = jnp.where(hit, -jnp.inf, gate)
        notsel = jnp.where(bidx == qblk, 0.0, 1.0 - sel)
        pad = jnp.zeros((LANES - nblk, TM), f32)
        if nblk == HEAD_DIM:
            ns = jnp.concatenate([pad, notsel], axis=0).T
        else:
            ns = jnp.concatenate([pad[:HEAD_DIM], notsel, pad[HEAD_DIM:]], axis=0).T
        q_ref[h] = (q * SCALE + ns).astype(bf16)
        k = aqk_ref[:, 512 + LANES * h:512 + LANES * (h + 1)]
        k_ref[h] = jnp.where(lane == HEAD_DIM + kblk, NEG, k).astype(bf16)
        v = av_ref[:, LANES * h:LANES * (h + 1)]
        v_ref[h] = jnp.where(lane == HEAD_DIM, 1.0, v.astype(f32)).astype(bf16)


def _moba_prep(aqk, av, km):
    s = aqk.shape[0]
    nblk = km.shape[0]
    hb = pl.BlockSpec((N_HEADS, TM, LANES), lambda i: (0, i, 0))
    out = jax.ShapeDtypeStruct((N_HEADS, s, LANES), bf16)
    return pl.pallas_call(
        functools.partial(_moba_prep_kernel, nblk=nblk),
        grid=(s // TM,),
        in_specs=[pl.BlockSpec((TM, 1024), lambda i: (i, 0)),
                  pl.BlockSpec((TM, 512), lambda i: (i, 0)),
                  pl.BlockSpec((nblk, 512), lambda i: (0, 0))],
        out_specs=[hb, hb, hb],
        out_shape=[out, out, out],
        compiler_params=_params("arbitrary"),
        name="moba_prep",
    )(aqk, av, km)


def _fox_prep_kernel(bqk_ref, bv_ref, fb_ref, bf_ref, q_ref, k_ref, v_ref, carry_ref):
    i = pl.program_id(0)

    @pl.when(i == 0)
    def _():
        carry_ref[...] = jnp.zeros_like(carry_ref)

    x = fb_ref[...] + bf_ref[...]
    logf = jnp.minimum(x, 0.0) - jnp.log1p(jnp.exp(-jnp.abs(x)))
    r = lax.broadcasted_iota(jnp.int32, (TM, TM), 0)
    c = lax.broadcasted_iota(jnp.int32, (TM, TM), 1)
    tri = jnp.where(c <= r, 1.0, 0.0).astype(f32)
    cum = jnp.dot(tri, logf, precision=_HI, preferred_element_type=f32) + carry_ref[...]
    carry_ref[...] = cum[TM - 1:TM, :]
    lane = lax.broadcasted_iota(jnp.int32, (TM, LANES), 1)
    d = HEAD_DIM
    for h in range(N_HEADS):
        cb = jnp.broadcast_to(cum[:, h:h + 1], (TM, LANES))
        hi = cb.astype(bf16).astype(f32)
        r1 = cb - hi
        mid = r1.astype(bf16).astype(f32)
        lo = r1 - mid
        q = bqk_ref[:, LANES * h:LANES * (h + 1)] * SCALE
        qa = jnp.where(lane < d, q,
             jnp.where(lane == d, hi,
             jnp.where(lane == d + 1, mid,
             jnp.where(lane == d + 2, lo,
             jnp.where(lane < d + 6, 1.0, 0.0)))))
        q_ref[h] = qa.astype(bf16)
        k = bqk_ref[:, 512 + LANES * h:512 + LANES * (h + 1)]
        ka = jnp.where(lane < d, k,
             jnp.where(lane < d + 3, 1.0,
             jnp.where(lane == d + 3, -hi,
             jnp.where(lane == d + 4, -mid,
             jnp.where(lane == d + 5, -lo, 0.0)))))
        k_ref[h] = ka.astype(bf16)
        v = bv_ref[:, LANES * h:LANES * (h + 1)]
        v_ref[h] = jnp.where(lane == d, 1.0, v.astype(f32)).astype(bf16)


def _fox_prep(bqk, bv, fb, bf_lane):
    s = bqk.shape[0]
    hb = pl.BlockSpec((N_HEADS, TM, LANES), lambda i: (0, i, 0))
    out = jax.ShapeDtypeStruct((N_HEADS, s, LANES), bf16)
    return pl.pallas_call(
        _fox_prep_kernel,
        grid=(s // TM,),
        in_specs=[pl.BlockSpec((TM, 1024), lambda i: (i, 0)),
                  pl.BlockSpec((TM, 512), lambda i: (i, 0)),
                  pl.BlockSpec((TM, LANES), lambda i: (i, 0)),
                  pl.BlockSpec((1, LANES), lambda i: (0, 0))],
        out_specs=[hb, hb, hb],
        out_shape=[out, out, out],
        scratch_shapes=[pltpu.VMEM((1, LANES), f32)],
        compiler_params=_params("arbitrary"),
        name="fox_prep",
    )(bqk, bv, fb, bf_lane)


def _flash_step(q, k_ref, v_ref, j, carry, add_bias):
    m, acc = carry
    off = pl.multiple_of(j * TA, TA)
    k = k_ref[0, pl.ds(off, TA), :]
    v = v_ref[0, pl.ds(off, TA), :]
    s = lax.dot_general(q, k, (((1,), (1,)), ((), ())), preferred_element_type=f32)
    s = add_bias(s)
    m_new = jnp.maximum(m, jnp.max(s, axis=-1, keepdims=True))
    alpha = jnp.exp(m - m_new)
    p = jnp.exp(s - m_new)
    acc = alpha * acc + jnp.dot(p.astype(bf16), v, preferred_element_type=f32)
    return m_new, acc


def _flash_finish(o_ref, acc):
    o_ref[0] = (acc / acc[:, HEAD_DIM:HEAD_DIM + 1]).astype(o_ref.dtype)


def _flash_init():
    return jnp.full((TA, 1), -jnp.inf, f32), jnp.zeros((TA, LANES), f32)


def _fox_flash_kernel(q_ref, k_ref, v_ref, o_ref):
    i = pl.program_id(1)
    q = q_ref[0]
    carry = lax.fori_loop(
        0, i, lambda j, c: _flash_step(q, k_ref, v_ref, j, c, lambda s: s), _flash_init())
    row = lax.broadcasted_iota(jnp.int32, (TA, TA), 0)
    col = lax.broadcasted_iota(jnp.int32, (TA, TA), 1)
    _, acc = _flash_step(q, k_ref, v_ref, i, carry, lambda s: jnp.where(col <= row, s, NEG))
    _flash_finish(o_ref, acc)


def _moba_flash_kernel(far_ref, q_ref, k_ref, v_ref, bias_ref, o_ref, *, near):
    h = pl.program_id(0)
    i = pl.program_id(1)
    q = q_ref[0]
    far = far_ref[h]
    n_far = jnp.maximum(i - (near - 1), 0)
    carry = lax.fori_loop(
        0, n_far, lambda j, c: _flash_step(q, k_ref, v_ref, j, c, lambda s: s + far), _flash_init())
    carry = lax.fori_loop(
        n_far, i + 1,
        lambda j, c: _flash_step(q, k_ref, v_ref, j, c, lambda s: s + bias_ref[0, i - j]), carry)
    _flash_finish(o_ref, carry[1])


def _flash_specs(s):
    qspec = pl.BlockSpec((1, TA, LANES), lambda h, i: (h, i, 0))
    kvspec = _resident((1, s, LANES), lambda h, i: (h, 0, 0))
    return qspec, kvspec


def _fox_flash(q, k, v):
    s = q.shape[1]
    qspec, kvspec = _flash_specs(s)
    return pl.pallas_call(
        _fox_flash_kernel,
        grid=(N_HEADS, s // TA),
        in_specs=[qspec, kvspec, kvspec],
        out_specs=qspec,
        out_shape=jax.ShapeDtypeStruct((N_HEADS, s, LANES), bf16),
        compiler_params=_params("arbitrary", "arbitrary"),
        name="fox_flash",
    )(q, k, v)


def _moba_flash(far, q, k, v, bias):
    s = q.shape[1]
    near = bias.shape[1]
    qspec, kvspec = _flash_specs(s)
    return pl.pallas_call(
        functools.partial(_moba_flash_kernel, near=near),
        grid=(N_HEADS, s // TA),
        in_specs=[pl.BlockSpec(memory_space=pltpu.SMEM), qspec, kvspec, kvspec,
                  _resident((1, near, TA, TA), lambda h, i: (h, 0, 0, 0))],
        out_specs=qspec,
        out_shape=jax.ShapeDtypeStruct((N_HEADS, s, LANES), bf16),
        compiler_params=_params("arbitrary", "arbitrary"),
        name="moba_flash",
    )(far, q, k, v, bias)


def _swa_kernel(cur_ref, prev_ref, bias_ref, sink_ref, o_ref):
    i = pl.program_id(0)
    first = jnp.where(i == 0, 1, 0)
    lane = lax.broadcasted_iota(jnp.int32, (Q_BLOCK, LANES), 1)
    qb = Q_BLOCK
    for b in range(TM // qb):
        if b == 0:
            kband = jnp.concatenate([prev_ref[TM - qb:TM, _C_K:_C_V], cur_ref[0:qb, _C_K:_C_V]], axis=0)
            vband = jnp.concatenate([prev_ref[TM - qb:TM, _C_V:_C_W], cur_ref[0:qb, _C_V:_C_W]], axis=0)
            variant = first
        else:
            kband = cur_ref[qb * (b - 1):qb * (b + 1), _C_K:_C_V]
            vband = cur_ref[qb * (b - 1):qb * (b + 1), _C_V:_C_W]
            variant = 0
        outs = []
        for g in range(C_HEADS // C_GROUP):
            qs = jnp.concatenate(
                [cur_ref[qb * b:qb * (b + 1), LANES * (C_GROUP * g + u):LANES * (C_GROUP * g + u + 1)]
                 for u in range(C_GROUP)], axis=0)
            s = lax.dot_general(qs, kband, (((1,), (1,)), ((), ())), preferred_element_type=f32)
            s = s + bias_ref[variant, g]
            sink = sink_ref[g]
            m = jnp.maximum(jnp.max(s, axis=-1, keepdims=True), sink)
            p = jnp.exp(s - m)
            den = jnp.sum(p, axis=-1, keepdims=True) + jnp.exp(sink - m)
            pb = p.astype(bf16)
            for u in range(C_GROUP):
                half = (C_GROUP * g + u) % 2
                vg = vband[:, 0:LANES] if g == half else vband[:, LANES:2 * LANES]
                pv = jnp.dot(pb[qb * u:qb * (u + 1)], vg, preferred_element_type=f32)
                outs.append(pv / den[qb * u:qb * (u + 1)])
        for pr in range(C_HEADS // 2):
            o_ref[qb * b:qb * (b + 1), LANES * pr:LANES * (pr + 1)] = jnp.where(
                lane < HEAD_DIM, outs[2 * pr], outs[2 * pr + 1]).astype(o_ref.dtype)


def _swa(c, bias, sink):
    s = c.shape[0]
    return pl.pallas_call(
        _swa_kernel,
        grid=(s // TM,),
        in_specs=[pl.BlockSpec((TM, _C_W), lambda i: (i, 0)),
                  pl.BlockSpec((TM, _C_W), lambda i: (jnp.maximum(i - 1, 0), 0)),
                  pl.BlockSpec(bias.shape, lambda i: (0, 0, 0, 0)),
                  pl.BlockSpec(sink.shape, lambda i: (0, 0, 0))],
        out_specs=pl.BlockSpec((TM, 512), lambda i: (i, 0)),
        out_shape=jax.ShapeDtypeStruct((s, 512), bf16),
        compiler_params=_params("arbitrary"),
        name="swa",
    )(c, c, bias, sink)


def _mix_kernel(x_ref, g_ref, wg_ref, oa_ref, ob_ref, oc_ref, wpa_ref, wpb_ref, wpc_ref, wo_ref,
                out_ref, mg_ref):
    x = x_ref[...]
    h = _rms(x, g_ref[...]).astype(bf16)
    oc = oc_ref[...]
    cw = 512
    for n in range(D_MODEL // cw):
        lo = cw * n

        def heads(o_ref, w_ref):
            acc = jnp.dot(o_ref[0], w_ref[0:LANES, lo:lo + cw], preferred_element_type=f32)
            for hh in range(1, N_HEADS):
                acc += jnp.dot(o_ref[hh], w_ref[LANES * hh:LANES * (hh + 1), lo:lo + cw],
                               preferred_element_type=f32)
            return acc

        def gate(which):
            z = jnp.dot(h, wg_ref[:, D_MODEL * which + lo:D_MODEL * which + lo + cw],
                        preferred_element_type=f32)
            return jax.nn.sigmoid(z)

        mg = gate(0) * heads(oa_ref, wpa_ref)
        mg += gate(1) * heads(ob_ref, wpb_ref)
        mg += gate(2) * jnp.dot(oc, wpc_ref[:, lo:lo + cw], preferred_element_type=f32)
        mg_ref[:, lo:lo + cw] = mg.astype(bf16)
    out_ref[...] = x + jnp.dot(mg_ref[...], wo_ref[...], preferred_element_type=f32)


def _mix(x2, g, wg, oa, ob, oc, wpa, wpb, wpc, wo):
    s = x2.shape[0]
    hb = pl.BlockSpec((N_HEADS, TM, LANES), lambda i: (0, i, 0))
    full = lambda a: _resident(a.shape, lambda i: (0,) * a.ndim)
    return pl.pallas_call(
        _mix_kernel,
        grid=(s // TM,),
        in_specs=[pl.BlockSpec((TM, D_MODEL), lambda i: (i, 0)),
                  pl.BlockSpec((1, D_MODEL), lambda i: (0, 0)),
                  full(wg), hb, hb,
                  pl.BlockSpec((TM, 512), lambda i: (i, 0)),
                  full(wpa), full(wpb), full(wpc), full(wo)],
        out_specs=pl.BlockSpec((TM, D_MODEL), lambda i: (i, 0)),
        out_shape=jax.ShapeDtypeStruct((s, D_MODEL), f32),
        scratch_shapes=[pltpu.VMEM((TM, D_MODEL), bf16)],
        compiler_params=_params("arbitrary"),
        name="mix",
    )(x2, g, wg, oa, ob, oc, wpa, wpb, wpc, wo)


def _ffn_kernel(x_ref, xh_ref, g_ref, wup_ref, cw_ref, cb_ref, wdn_ref, gf_ref, out_ref,
                u_ref, act_ref, *, final):
    i = pl.program_id(0)
    x = x_ref[...]
    halo = jnp.where(i > 0, xh_ref[...], 0.0)
    h2 = _rms(jnp.concatenate([halo, x], axis=0), g_ref[...]).astype(bf16)

    def conv(lo):
        u_ref[...] = jnp.dot(h2, wup_ref[:, lo:lo + FF_CHUNK], preferred_element_type=f32)
        w = cw_ref[:, lo:lo + FF_CHUNK]
        return (w[2:3] * u_ref[HALO:HALO + TM] + w[1:2] * u_ref[HALO - 1:HALO - 1 + TM]
                + w[0:1] * u_ref[HALO - 2:HALO - 2 + TM] + cb_ref[:, lo:lo + FF_CHUNK])

    for c in range(D_FF // FF_CHUNK):
        a = conv(FF_CHUNK * c)
        b = conv(D_FF + FF_CHUNK * c)
        act_ref[:, FF_CHUNK * c:FF_CHUNK * (c + 1)] = (a * jax.nn.sigmoid(a) * b).astype(bf16)
    o = x + jnp.dot(act_ref[...], wdn_ref[...], preferred_element_type=f32)
    if final:
        o = _rms(o, gf_ref[...])
    out_ref[...] = o


def _ffn(x2, g, wup, cw, cb, wdn, gf, final):
    s = x2.shape[0]
    full = lambda a: _resident(a.shape, lambda i: (0,) * a.ndim)
    vec = lambda a: pl.BlockSpec(a.shape, lambda i: (0,) * a.ndim)
    return pl.pallas_call(
        functools.partial(_ffn_kernel, final=final),
        grid=(s // TM,),
        in_specs=[pl.BlockSpec((TM, D_MODEL), lambda i: (i, 0)),
                  pl.BlockSpec((HALO, D_MODEL), lambda i: (jnp.maximum(i * (TM // HALO) - 1, 0), 0)),
                  vec(g), full(wup), vec(cw), vec(cb), full(wdn), vec(gf)],
        out_specs=pl.BlockSpec((TM, D_MODEL), lambda i: (i, 0)),
        out_shape=jax.ShapeDtypeStruct((s, D_MODEL), f32),
        scratch_shapes=[pltpu.VMEM((TM + HALO, FF_CHUNK), f32), pltpu.VMEM((TM, D_FF), bf16)],
        compiler_params=_params("arbitrary"),
        name="ffn",
    )(x2, x2, g, wup, cw, cb, wdn, gf)


def _pad_heads(w, nh):
    l, d, _ = w.shape
    w = jnp.pad(w.reshape(l, d, nh, HEAD_DIM), ((0, 0), (0, 0), (0, 0), (0, LANES - HEAD_DIM)))
    return w.reshape(l, d, nh * LANES)


def _pad_head_rows(w, nh):
    l, _, d = w.shape
    w = jnp.pad(w.reshape(l, nh, HEAD_DIM, d), ((0, 0), (0, 0), (0, LANES - HEAD_DIM), (0, 0)))
    return w.reshape(l, nh * LANES, d)


def _fused_in_weight(w_in):
    l = w_in.shape[0]
    cuts = np.cumsum([0, 256, 256, 256, 256, 256, 256, N_HEADS, 512, 128, 128]).tolist()
    qa, ka, va, qb, kb, vb, fb, qc, kc, vc = (w_in[..., cuts[n]:cuts[n + 1]] for n in range(10))
    qc = (qc * SCALE).reshape(l, D_MODEL, C_HEADS, HEAD_DIM)
    zero = jnp.zeros_like(qc[:, :, 0])
    qc_pad = jnp.concatenate(
        [jnp.concatenate([qc[:, :, hq], zero] if hq < C_GROUP else [zero, qc[:, :, hq]], axis=-1)
         for hq in range(C_HEADS)], axis=-1)
    vc2 = jnp.concatenate([vc, vc[..., HEAD_DIM:], vc[..., :HEAD_DIM]], axis=-1)
    fbp = jnp.pad(fb, ((0, 0), (0, 0), (0, LANES - N_HEADS)))
    w1 = jnp.concatenate([_pad_heads(qa, N_HEADS), _pad_heads(ka, N_HEADS), _pad_heads(va, N_HEADS),
                          _pad_heads(qb, N_HEADS), _pad_heads(kb, N_HEADS), _pad_heads(vb, N_HEADS),
                          qc_pad, kc, vc2, fbp], axis=-1)
    assert w1.shape[-1] == _W1_COLS
    return w1.astype(bf16), w_in[..., cuts[10]:].astype(bf16)


def _moba_bias_tiles(tbl_a, s):
    near = min(-(-(MAX_DISTANCE + TA - 1) // TA), s // TA)
    dist = jnp.arange(near * TA, dtype=jnp.int32)
    by_dist = tbl_a[:, _t5_bucket(dist)]
    idx = (np.arange(near)[:, None, None] * TA + np.arange(TA)[None, :, None]
           - np.arange(TA)[None, None, :])
    tiles = jnp.where(idx >= 0, by_dist[:, np.maximum(idx, 0)], NEG)
    far = tbl_a[:, _t5_bucket(jnp.asarray([near * TA], jnp.int32))[0]]
    return tiles.astype(f32), far.astype(f32)


def _swa_tables(tbl_c, sinks):
    tl = np.arange(Q_BLOCK)
    sl = np.arange(2 * Q_BLOCK)
    dist = tl[:, None] + Q_BLOCK - sl[None, :]
    bias = tbl_c[:, _t5_bucket(jnp.asarray(dist, jnp.int32))]
    band = (dist >= 0) & (dist < WINDOW)
    masks = np.stack([band, band & (sl[None, :] >= Q_BLOCK)])
    table = jnp.where(masks[:, None], bias[None], NEG)
    table = table.reshape(2, C_HEADS // C_GROUP, C_GROUP * Q_BLOCK, 2 * Q_BLOCK).astype(f32)
    sink = jnp.repeat(sinks.astype(f32), Q_BLOCK, axis=-1)
    sink = sink.reshape(sinks.shape[0], C_HEADS // C_GROUP, C_GROUP * Q_BLOCK, 1)
    return table, sink


def kernel(x, ln1, w_in, b_f, sinks, w_pa, w_pb, w_pc, w_o, ln2, w_up, conv_w, conv_b, w_down,
           rel_bias, ln_f):
    bsz, s, _ = x.shape
    assert bsz == 1 and s % TM == 0 and s % TA == 0 and TM % MOBA_BLOCK == 0
    assert s // MOBA_BLOCK <= HEAD_DIM
    w1, wg = _fused_in_weight(w_in)
    wpa = _pad_head_rows(w_pa, N_HEADS).astype(bf16)
    wpb = _pad_head_rows(w_pb, N_HEADS).astype(bf16)
    wpc, wo, wup, wdn = (w.astype(bf16) for w in (w_pc, w_o, w_up, w_down))
    tbl_a = rel_bias[:, :N_HEADS].T
    tbl_c = rel_bias[:, N_HEADS:].T
    moba_tiles, moba_far = _moba_bias_tiles(tbl_a, s)
    swa_bias, swa_sink = _swa_tables(tbl_c, sinks)
    bf_lane = jnp.pad(b_f.astype(f32), ((0, 0), (0, LANES - N_HEADS)))[:, None, :]
    gf = ln_f.astype(f32)[None, :]

    x2 = x[0]
    for l in range(DEPTH):
        aqk, av, bqk, bv, c, fb, km = _inproj(x2, ln1[l][None, :], w1[l])
        km = km.reshape(s // MOBA_BLOCK, 512)
        qa, ka, va = _moba_prep(aqk, av, km)
        qb, kb, vb = _fox_prep(bqk, bv, fb, bf_lane[l])
        oa = _moba_flash(moba_far, qa, ka, va, moba_tiles)
        ob = _fox_flash(qb, kb, vb)
        oc = _swa(c, swa_bias, swa_sink[l])
        x2 = _mix(x2, ln1[l][None, :], wg[l], oa, ob, oc, wpa[l], wpb[l], wpc[l], wo[l])
        x2 = _ffn(x2, ln2[l][None, :], wup[l], conv_w[l], conv_b[l][None, :], wdn[l], gf,
                  final=(l == DEPTH - 1))
    return x2[None]
```

```python
import functools
import math

import numpy as np
import jax
import jax.numpy as jnp
from jax import lax
from jax.experimental import pallas as pl
from jax.experimental.pallas import tpu as pltpu

D_MODEL = 1024
DEPTH = 4
HEAD_DIM = 64
N_HEADS = 4
C_HEADS = 8
C_GROUP = 4
MOBA_BLOCK = 256
MOBA_TOPK = 3
Q_BLOCK = 128
WINDOW = 128
NUM_BUCKETS = 32
MAX_DISTANCE = 4096
D_FF = 2816
EPS = 1e-6
NEG = -1e30
SCALE = HEAD_DIM ** -0.5

LANES = 128
HALO = 8
TM = 512
TA = 512
FF_CHUNK = 256
VMEM_LIMIT = 56 * 1024 * 1024

_A_QK = (0, 1024)
_A_V = (1024, 1536)
_B_QK = (1536, 2560)
_B_V = (2560, 3072)
_C_ALL = (3072, 4480)
_F_B = (4480, 4608)
_W1_COLS = 4608
_C_K = 1024
_C_V = 1152
_C_W = 1408

bf16 = jnp.bfloat16
f32 = jnp.float32
_HI = lax.Precision.HIGHEST


def _rms(x, g):
    ms = jnp.mean(x * x, axis=-1, keepdims=True)
    return x * lax.rsqrt(ms + EPS) * g


def _t5_bucket(dist):
    dist = jnp.maximum(dist, 0)
    max_exact = NUM_BUCKETS // 2
    d = jnp.maximum(dist.astype(jnp.float32), 1.0)
    large = max_exact + (jnp.log(d / max_exact) / math.log(MAX_DISTANCE / max_exact)
                         * (NUM_BUCKETS - max_exact)).astype(jnp.int32)
    large = jnp.minimum(large, NUM_BUCKETS - 1)
    return jnp.where(dist < max_exact, dist, large)


def _params(*sem):
    return pltpu.CompilerParams(dimension_semantics=sem, vmem_limit_bytes=VMEM_LIMIT)


def _resident(shape, index_map):
    return pl.BlockSpec(shape, index_map, pipeline_mode=pl.Buffered(1))


def _inproj_kernel(x_ref, g_ref, w_ref, aqk_ref, av_ref, bqk_ref, bv_ref, c_ref, fb_ref, km_ref):
    h = _rms(x_ref[...], g_ref[...]).astype(bf16)

    def mm(span):
        return jnp.dot(h, w_ref[:, span[0]:span[1]], preferred_element_type=f32)

    aqk = mm(_A_QK)
    aqk_ref[...] = aqk
    ka = aqk[:, 512:1024]
    km_ref[0] = jnp.mean(ka.reshape(TM // MOBA_BLOCK, MOBA_BLOCK, 512), axis=1)
    av_ref[...] = mm(_A_V).astype(bf16)
    bqk_ref[...] = mm(_B_QK)
    bv_ref[...] = mm(_B_V).astype(bf16)
    c_ref[...] = mm(_C_ALL).astype(bf16)
    fb_ref[...] = mm(_F_B)


def _inproj(x2, g, w1):
    s = x2.shape[0]
    nt = s // TM
    row = lambda w: pl.BlockSpec((TM, w), lambda i: (i, 0))
    return pl.pallas_call(
        _inproj_kernel,
        grid=(nt,),
        in_specs=[row(D_MODEL),
                  pl.BlockSpec((1, D_MODEL), lambda i: (0, 0)),
                  _resident((D_MODEL, _W1_COLS), lambda i: (0, 0))],
        out_specs=[row(1024), row(512), row(1024), row(512), row(_C_W), row(LANES),
                   pl.BlockSpec((1, TM // MOBA_BLOCK, 512), lambda i: (i, 0, 0))],
        out_shape=[jax.ShapeDtypeStruct((s, 1024), f32),
                   jax.ShapeDtypeStruct((s, 512), bf16),
                   jax.ShapeDtypeStruct((s, 1024), f32),
                   jax.ShapeDtypeStruct((s, 512), bf16),
                   jax.ShapeDtypeStruct((s, _C_W), bf16),
                   jax.ShapeDtypeStruct((s, LANES), f32),
                   jax.ShapeDtypeStruct((nt, TM // MOBA_BLOCK, 512), f32)],
        compiler_params=_params("arbitrary"),
        name="inproj",
    )(x2, g, w1)


def _moba_prep_kernel(aqk_ref, av_ref, km_ref, q_ref, k_ref, v_ref, *, nblk):
    i = pl.program_id(0)
    lane = lax.broadcasted_iota(jnp.int32, (TM, LANES), 1)
    kblk = (i * TM + lax.broadcasted_iota(jnp.int32, (TM, LANES), 0)) >> 8
    bidx = lax.broadcasted_iota(jnp.int32, (nblk, TM), 0)
    bidx_f = bidx.astype(f32)
    qblk = (i * TM + lax.broadcasted_iota(jnp.int32, (nblk, TM), 1)) >> 8
    for h in range(N_HEADS):
        q = aqk_ref[:, LANES * h:LANES * (h + 1)]
        km = km_ref[:, LANES * h:LANES * (h + 1)]
        gate = lax.dot_general(km, q, (((1,), (1,)), ((), ())), precision=_HI,
                               preferred_element_type=f32)
        gate = jnp.where(bidx < qblk, gate, -jnp.inf)
        sel = jnp.zeros((nblk, TM), f32)
        for _ in range(MOBA_TOPK):
            m = jnp.max(gate, axis=0, keepdims=True)
            first = jnp.min(jnp.where(gate == m, bidx_f, float(nblk)), axis=0, keepdims=True)
            hit = jnp.logical_and(bidx_f == first, m > -jnp.inf)
            sel = jnp.where(hit, 1.0, sel)
            gate = jnp.where(hit, -jnp.inf, gate)
        notsel = jnp.where(bidx == qblk, 0.0, 1.0 - sel)
        pad = jnp.zeros((LANES - nblk, TM), f32)
        if nblk == HEAD_DIM:
            ns = jnp.concatenate([pad, notsel], axis=0).T
        else:
            ns = jnp.concatenate([pad[:HEAD_DIM], notsel, pad[HEAD_DIM:]], axis=0).T
        q_ref[h] = (q * SCALE + ns).astype(bf16)
        k = aqk_ref[:, 512 + LANES * h:512 + LANES * (h + 1)]
        k_ref[h] = jnp.where(lane == HEAD_DIM + kblk, NEG, k).astype(bf16)
        v = av_ref[:, LANES * h:LANES * (h + 1)]
        v_ref[h] = jnp.where(lane == HEAD_DIM, 1.0, v.astype(f32)).astype(bf16)


def _moba_prep(aqk, av, km):
    s = aqk.shape[0]
    nblk = km.shape[0]
    hb = pl.BlockSpec((N_HEADS, TM, LANES), lambda i: (0, i, 0))
    out = jax.ShapeDtypeStruct((N_HEADS, s, LANES), bf16)
    return pl.pallas_call(
        functools.partial(_moba_prep_kernel, nblk=nblk),
        grid=(s // TM,),
        in_specs=[pl.BlockSpec((TM, 1024), lambda i: (i, 0)),
                  pl.BlockSpec((TM, 512), lambda i: (i, 0)),
                  pl.BlockSpec((nblk, 512), lambda i: (0, 0))],
        out_specs=[hb, hb, hb],
        out_shape=[out, out, out],
        compiler_params=_params("arbitrary"),
        name="moba_prep",
    )(aqk, av, km)


def _fox_prep_kernel(bqk_ref, bv_ref, fb_ref, bf_ref, q_ref, k_ref, v_ref, carry_ref):
    i = pl.program_id(0)

    @pl.when(i == 0)
    def _():
        carry_ref[...] = jnp.zeros_like(carry_ref)

    x = fb_ref[...] + bf_ref[...]
    logf = jnp.minimum(x, 0.0) - jnp.log1p(jnp.exp(-jnp.abs(x)))
    r = lax.broadcasted_iota(jnp.int32, (TM, TM), 0)
    c = lax.broadcasted_iota(jnp.int32, (TM, TM), 1)
    tri = jnp.where(c <= r, 1.0, 0.0).astype(f32)
    cum = jnp.dot(tri, logf, precision=_HI, preferred_element_type=f32) + carry_ref[...]
    carry_ref[...] = cum[TM - 1:TM, :]
    lane = lax.broadcasted_iota(jnp.int32, (TM, LANES), 1)
    d = HEAD_DIM
    for h in range(N_HEADS):
        cb = jnp.broadcast_to(cum[:, h:h + 1], (TM, LANES))
        hi = cb.astype(bf16).astype(f32)
        r1 = cb - hi
        mid = r1.astype(bf16).astype(f32)
        lo = r1 - mid
        q = bqk_ref[:, LANES * h:LANES * (h + 1)] * SCALE
        qa = jnp.where(lane < d, q,
             jnp.where(lane == d, hi,
             jnp.where(lane == d + 1, mid,
             jnp.where(lane == d + 2, lo,
             jnp.where(lane < d + 6, 1.0, 0.0)))))
        q_ref[h] = qa.astype(bf16)
        k = bqk_ref[:, 512 + LANES * h:512 + LANES * (h + 1)]
        ka = jnp.where(lane < d, k,
             jnp.where(lane < d + 3, 1.0,
             jnp.where(lane == d + 3, -hi,
             jnp.where(lane == d + 4, -mid,
             jnp.where(lane == d + 5, -lo, 0.0)))))
        k_ref[h] = ka.astype(bf16)
        v = bv_ref[:, LANES * h:LANES * (h + 1)]
        v_ref[h] = jnp.where(lane == d, 1.0, v.astype(f32)).astype(bf16)


def _fox_prep(bqk, bv, fb, bf_lane):
    s = bqk.shape[0]
    hb = pl.BlockSpec((N_HEADS, TM, LANES), lambda i: (0, i, 0))
    out = jax.ShapeDtypeStruct((N_HEADS, s, LANES), bf16)
    return pl.pallas_call(
        _fox_prep_kernel,
        grid=(s // TM,),
        in_specs=[pl.BlockSpec((TM, 1024), lambda i: (i, 0)),
                  pl.BlockSpec((TM, 512), lambda i: (i, 0)),
                  pl.BlockSpec((TM, LANES), lambda i: (i, 0)),
                  pl.BlockSpec((1, LANES), lambda i: (0, 0))],
        out_specs=[hb, hb, hb],
        out_shape=[out, out, out],
        scratch_shapes=[pltpu.VMEM((1, LANES), f32)],
        compiler_params=_params("arbitrary"),
        name="fox_prep",
    )(bqk, bv, fb, bf_lane)


def _flash_step(q, k_ref, v_ref, j, carry, add_bias):
    m, acc = carry
    off = pl.multiple_of(j * TA, TA)
    k = k_ref[0, pl.ds(off, TA), :]
    v = v_ref[0, pl.ds(off, TA), :]
    s = lax.dot_general(q, k, (((1,), (1,)), ((), ())), preferred_element_type=f32)
    s = add_bias(s)
    m_new = jnp.maximum(m, jnp.max(s, axis=-1, keepdims=True))
    alpha = jnp.exp(m - m_new)
    p = jnp.exp(s - m_new)
    acc = alpha * acc + jnp.dot(p.astype(bf16), v, preferred_element_type=f32)
    return m_new, acc


def _flash_finish(o_ref, acc):
    o_ref[0] = (acc / acc[:, HEAD_DIM:HEAD_DIM + 1]).astype(o_ref.dtype)


def _flash_init():
    return jnp.full((TA, 1), -jnp.inf, f32), jnp.zeros((TA, LANES), f32)


def _fox_flash_kernel(q_ref, k_ref, v_ref, o_ref):
    i = pl.program_id(1)
    q = q_ref[0]
    carry = lax.fori_loop(
        0, i, lambda j, c: _flash_step(q, k_ref, v_ref, j, c, lambda s: s), _flash_init())
    row = lax.broadcasted_iota(jnp.int32, (TA, TA), 0)
    col = lax.broadcasted_iota(jnp.int32, (TA, TA), 1)
    _, acc = _flash_step(q, k_ref, v_ref, i, carry, lambda s: jnp.where(col <= row, s, NEG))
    _flash_finish(o_ref, acc)


def _moba_flash_kernel(far_ref, q_ref, k_ref, v_ref, bias_ref, o_ref, *, near):
    h = pl.program_id(0)
    i = pl.program_id(1)
    q = q_ref[0]
    far = far_ref[h]
    n_far = jnp.maximum(i - (near - 1), 0)
    carry = lax.fori_loop(
        0, n_far, lambda j, c: _flash_step(q, k_ref, v_ref, j, c, lambda s: s + far), _flash_init())
    carry = lax.fori_loop(
        n_far, i + 1,
        lambda j, c: _flash_step(q, k_ref, v_ref, j, c, lambda s: s + bias_ref[0, i - j]), carry)
    _flash_finish(o_ref, carry[1])


def _flash_specs(s):
    qspec = pl.BlockSpec((1, TA, LANES), lambda h, i: (h, i, 0))
    kvspec = _resident((1, s, LANES), lambda h, i: (h, 0, 0))
    return qspec, kvspec


def _fox_flash(q, k, v):
    s = q.shape[1]
    qspec, kvspec = _flash_specs(s)
    return pl.pallas_call(
        _fox_flash_kernel,
        grid=(N_HEADS, s // TA),
        in_specs=[qspec, kvspec, kvspec],
        out_specs=qspec,
        out_shape=jax.ShapeDtypeStruct((N_HEADS, s, LANES), bf16),
        compiler_params=_params("arbitrary", "arbitrary"),
        name="fox_flash",
    )(q, k, v)


def _moba_flash(far, q, k, v, bias):
    s = q.shape[1]
    near = bias.shape[1]
    qspec, kvspec = _flash_specs(s)
    return pl.pallas_call(
        functools.partial(_moba_flash_kernel, near=near),
        grid=(N_HEADS, s // TA),
        in_specs=[pl.BlockSpec(memory_space=pltpu.SMEM), qspec, kvspec, kvspec,
                  _resident((1, near, TA, TA), lambda h, i: (h, 0, 0, 0))],
        out_specs=qspec,
        out_shape=jax.ShapeDtypeStruct((N_HEADS, s, LANES), bf16),
        compiler_params=_params("arbitrary", "arbitrary"),
        name="moba_flash",
    )(far, q, k, v, bias)


def _swa_kernel(cur_ref, prev_ref, bias_ref, sink_ref, o_ref):
    i = pl.program_id(0)
    first = jnp.where(i == 0, 1, 0)
    lane = lax.broadcasted_iota(jnp.int32, (Q_BLOCK, LANES), 1)
    qb = Q_BLOCK
    for b in range(TM // qb):
        if b == 0:
            kband = jnp.concatenate([prev_ref[TM - qb:TM, _C_K:_C_V], cur_ref[0:qb, _C_K:_C_V]], axis=0)
            vband = jnp.concatenate([prev_ref[TM - qb:TM, _C_V:_C_W], cur_ref[0:qb, _C_V:_C_W]], axis=0)
            variant = first
        else:
            kband = cur_ref[qb * (b - 1):qb * (b + 1), _C_K:_C_V]
            vband = cur_ref[qb * (b - 1):qb * (b + 1), _C_V:_C_W]
            variant = 0
        outs = []
        for g in range(C_HEADS // C_GROUP):
            qs = jnp.concatenate(
                [cur_ref[qb * b:qb * (b + 1), LANES * (C_GROUP * g + u):LANES * (C_GROUP * g + u + 1)]
                 for u in range(C_GROUP)], axis=0)
            s = lax.dot_general(qs, kband, (((1,), (1,)), ((), ())), preferred_element_type=f32)
            s = s + bias_ref[variant, g]
            sink = sink_ref[g]
            m = jnp.maximum(jnp.max(s, axis=-1, keepdims=True), sink)
            p = jnp.exp(s - m)
            den = jnp.sum(p, axis=-1, keepdims=True) + jnp.exp(sink - m)
            pb = p.astype(bf16)
            for u in range(C_GROUP):
                half = (C_GROUP * g + u) % 2
                vg = vband[:, 0:LANES] if g == half else vband[:, LANES:2 * LANES]
                pv = jnp.dot(pb[qb * u:qb * (u + 1)], vg, preferred_element_type=f32)
                outs.append(pv / den[qb * u:qb * (u + 1)])
        for pr in range(C_HEADS // 2):
            o_ref[qb * b:qb * (b + 1), LANES * pr:LANES * (pr + 1)] = jnp.where(
                lane < HEAD_DIM, outs[2 * pr], outs[2 * pr + 1]).astype(o_ref.dtype)


def _swa(c, bias, sink):
    s = c.shape[0]
    return pl.pallas_call(
        _swa_kernel,
        grid=(s // TM,),
        in_specs=[pl.BlockSpec((TM, _C_W), lambda i: (i, 0)),
                  pl.BlockSpec((TM, _C_W), lambda i: (jnp.maximum(i - 1, 0), 0)),
                  pl.BlockSpec(bias.shape, lambda i: (0, 0, 0, 0)),
                  pl.BlockSpec(sink.shape, lambda i: (0, 0, 0))],
        out_specs=pl.BlockSpec((TM, 512), lambda i: (i, 0)),
        out_shape=jax.ShapeDtypeStruct((s, 512), bf16),
        compiler_params=_params("arbitrary"),
        name="swa",
    )(c, c, bias, sink)


def _mix_kernel(x_ref, g_ref, wg_ref, oa_ref, ob_ref, oc_ref, wpa_ref, wpb_ref, wpc_ref, wo_ref,
                out_ref, mg_ref):
    x = x_ref[...]
    h = _rms(x, g_ref[...]).astype(bf16)
    oc = oc_ref[...]
    cw = 512
    for n in range(D_MODEL // cw):
        lo = cw * n

        def heads(o_ref, w_ref):
            acc = jnp.dot(o_ref[0], w_ref[0:LANES, lo:lo + cw], preferred_element_type=f32)
            for hh in range(1, N_HEADS):
                acc += jnp.dot(o_ref[hh], w_ref[LANES * hh:LANES * (hh + 1), lo:lo + cw],
                               preferred_element_type=f32)
            return acc

        def gate(which):
            z = jnp.dot(h, wg_ref[:, D_MODEL * which + lo:D_MODEL * which + lo + cw],
                        preferred_element_type=f32)
            return jax.nn.sigmoid(z)

        mg = gate(0) * heads(oa_ref, wpa_ref)
        mg += gate(1) * heads(ob_ref, wpb_ref)
        mg += gate(2) * jnp.dot(oc, wpc_ref[:, lo:lo + cw], preferred_element_type=f32)
        mg_ref[:, lo:lo + cw] = mg.astype(bf16)
    out_ref[...] = x + jnp.dot(mg_ref[...], wo_ref[...], preferred_element_type=f32)


def _mix(x2, g, wg, oa, ob, oc, wpa, wpb, wpc, wo):
    s = x2.shape[0]
    hb = pl.BlockSpec((N_HEADS, TM, LANES), lambda i: (0, i, 0))
    full = lambda a: _resident(a.shape, lambda i: (0,) * a.ndim)
    return pl.pallas_call(
        _mix_kernel,
        grid=(s // TM,),
        in_specs=[pl.BlockSpec((TM, D_MODEL), lambda i: (i, 0)),
                  pl.BlockSpec((1, D_MODEL), lambda i: (0, 0)),
                  full(wg), hb, hb,
                  pl.BlockSpec((TM, 512), lambda i: (i, 0)),
                  full(wpa), full(wpb), full(wpc), full(wo)],
        out_specs=pl.BlockSpec((TM, D_MODEL), lambda i: (i, 0)),
        out_shape=jax.ShapeDtypeStruct((s, D_MODEL), f32),
        scratch_shapes=[pltpu.VMEM((TM, D_MODEL), bf16)],
        compiler_params=_params("arbitrary"),
        name="mix",
    )(x2, g, wg, oa, ob, oc, wpa, wpb, wpc, wo)


def _ffn_kernel(x_ref, xh_ref, g_ref, wup_ref, cw_ref, cb_ref, wdn_ref, gf_ref, out_ref,
                u_ref, act_ref, *, final):
    i = pl.program_id(0)
    x = x_ref[...]
    halo = jnp.where(i > 0, xh_ref[...], 0.0)
    h2 = _rms(jnp.concatenate([halo, x], axis=0), g_ref[...]).astype(bf16)

    def conv(lo):
        u_ref[...] = jnp.dot(h2, wup_ref[:, lo:lo + FF_CHUNK], preferred_element_type=f32)
        w = cw_ref[:, lo:lo + FF_CHUNK]
        return (w[2:3] * u_ref[HALO:HALO + TM] + w[1:2] * u_ref[HALO - 1:HALO - 1 + TM]
                + w[0:1] * u_ref[HALO - 2:HALO - 2 + TM] + cb_ref[:, lo:lo + FF_CHUNK])

    for c in range(D_FF // FF_CHUNK):
        a = conv(FF_CHUNK * c)
        b = conv(D_FF + FF_CHUNK * c)
        act_ref[:, FF_CHUNK * c:FF_CHUNK * (c + 1)] = (a * jax.nn.sigmoid(a) * b).astype(bf16)
    o = x + jnp.dot(act_ref[...], wdn_ref[...], preferred_element_type=f32)
    if final:
        o = _rms(o, gf_ref[...])
    out_ref[...] = o


def _ffn(x2, g, wup, cw, cb, wdn, gf, final):
    s = x2.shape[0]
    full = lambda a: _resident(a.shape, lambda i: (0,) * a.ndim)
    vec = lambda a: pl.BlockSpec(a.shape, lambda i: (0,) * a.ndim)
    return pl.pallas_call(
        functools.partial(_ffn_kernel, final=final),
        grid=(s // TM,),
        in_specs=[pl.BlockSpec((TM, D_MODEL), lambda i: (i, 0)),
                  pl.BlockSpec((HALO, D_MODEL), lambda i: (jnp.maximum(i * (TM // HALO) - 1, 0), 0)),
                  vec(g), full(wup), vec(cw), vec(cb), full(wdn), vec(gf)],
        out_specs=pl.BlockSpec((TM, D_MODEL), lambda i: (i, 0)),
        out_shape=jax.ShapeDtypeStruct((s, D_MODEL), f32),
        scratch_shapes=[pltpu.VMEM((TM + HALO, FF_CHUNK), f32), pltpu.VMEM((TM, D_FF), bf16)],
        compiler_params=_params("arbitrary"),
        name="ffn",
    )(x2, x2, g, wup, cw, cb, wdn, gf)


def _pad_heads(w, nh):
    l, d, _ = w.shape
    w = jnp.pad(w.reshape(l, d, nh, HEAD_DIM), ((0, 0), (0, 0), (0, 0), (0, LANES - HEAD_DIM)))
    return w.reshape(l, d, nh * LANES)


def _pad_head_rows(w, nh):
    l, _, d = w.shape
    w = jnp.pad(w.reshape(l, nh, HEAD_DIM, d), ((0, 0), (0, 0), (0, LANES - HEAD_DIM), (0, 0)))
    return w.reshape(l, nh * LANES, d)


def _fused_in_weight(w_in):
    l = w_in.shape[0]
    cuts = np.cumsum([0, 256, 256, 256, 256, 256, 256, N_HEADS, 512, 128, 128]).tolist()
    qa, ka, va, qb, kb, vb, fb, qc, kc, vc = (w_in[..., cuts[n]:cuts[n + 1]] for n in range(10))
    qc = (qc * SCALE).reshape(l, D_MODEL, C_HEADS, HEAD_DIM)
    zero = jnp.zeros_like(qc[:, :, 0])
    qc_pad = jnp.concatenate(
        [jnp.concatenate([qc[:, :, hq], zero] if hq < C_GROUP else [zero, qc[:, :, hq]], axis=-1)
         for hq in range(C_HEADS)], axis=-1)
    vc2 = jnp.concatenate([vc, vc[..., HEAD_DIM:], vc[..., :HEAD_DIM]], axis=-1)
    fbp = jnp.pad(fb, ((0, 0), (0, 0), (0, LANES - N_HEADS)))
    w1 = jnp.concatenate([_pad_heads(qa, N_HEADS), _pad_heads(ka, N_HEADS), _pad_heads(va, N_HEADS),
                          _pad_heads(qb, N_HEADS), _pad_heads(kb, N_HEADS), _pad_heads(vb, N_HEADS),
                          qc_pad, kc, vc2, fbp], axis=-1)
    assert w1.shape[-1] == _W1_COLS
    return w1.astype(bf16), w_in[..., cuts[10]:].astype(bf16)


def _toeplitz_kernel(u_ref, o_ref):
    x = jnp.broadcast_to(u_ref[0, 0], (TA, 2 * TA))
    o_ref[0, 0] = pltpu.roll(x, 0, 1, stride=1, stride_axis=0)[:, :TA]


def _moba_bias_tiles(tbl_a, s):
    near = min(-(-(MAX_DISTANCE + TA - 1) // TA), s // TA)
    dist = jnp.arange(near * TA, dtype=jnp.int32)
    by_dist = tbl_a[:, _t5_bucket(dist)]
    m = np.arange(2 * TA)
    rel = np.where(m < TA, -m, np.where(m == TA, 0, 2 * TA - m))
    idx = np.arange(near)[:, None] * TA + rel[None, :]
    gen = jnp.where(idx >= 0, by_dist[:, np.maximum(idx, 0)], NEG).astype(f32)
    tiles = pl.pallas_call(
        _toeplitz_kernel,
        grid=(N_HEADS, near),
        in_specs=[pl.BlockSpec((1, 1, 1, 2 * TA), lambda h, d: (h, d, 0, 0))],
        out_specs=pl.BlockSpec((1, 1, TA, TA), lambda h, d: (h, d, 0, 0)),
        out_shape=jax.ShapeDtypeStruct((N_HEADS, near, TA, TA), f32),
        compiler_params=_params("arbitrary", "arbitrary"),
        name="moba_bias_tiles",
    )(gen[:, :, None, :])
    far = tbl_a[:, _t5_bucket(jnp.asarray([near * TA], jnp.int32))[0]]
    return tiles, far.astype(f32)


def _swa_tables(tbl_c, sinks):
    tl = np.arange(Q_BLOCK)
    sl = np.arange(2 * Q_BLOCK)
    dist = tl[:, None] + Q_BLOCK - sl[None, :]
    bias = tbl_c[:, _t5_bucket(jnp.asarray(dist, jnp.int32))]
    band = (dist >= 0) & (dist < WINDOW)
    masks = np.stack([band, band & (sl[None, :] >= Q_BLOCK)])
    table = jnp.where(masks[:, None], bias[None], NEG)
    table = table.reshape(2, C_HEADS // C_GROUP, C_GROUP * Q_BLOCK, 2 * Q_BLOCK).astype(f32)
    sink = jnp.repeat(sinks.astype(f32), Q_BLOCK, axis=-1)
    sink = sink.reshape(sinks.shape[0], C_HEADS // C_GROUP, C_GROUP * Q_BLOCK, 1)
    return table, sink


def kernel(x, ln1, w_in, b_f, sinks, w_pa, w_pb, w_pc, w_o, ln2, w_up, conv_w, conv_b, w_down,
           rel_bias, ln_f):
    bsz, s, _ = x.shape
    assert bsz == 1 and s % TM == 0 and s % TA == 0 and TM % MOBA_BLOCK == 0
    assert s // MOBA_BLOCK <= HEAD_DIM
    w1, wg = _fused_in_weight(w_in)
    wpa = _pad_head_rows(w_pa, N_HEADS).astype(bf16)
    wpb = _pad_head_rows(w_pb, N_HEADS).astype(bf16)
    wpc, wo, wup, wdn = (w.astype(bf16) for w in (w_pc, w_o, w_up, w_down))
    tbl_a = rel_bias[:, :N_HEADS].T
    tbl_c = rel_bias[:, N_HEADS:].T
    moba_tiles, moba_far = _moba_bias_tiles(tbl_a, s)
    swa_bias, swa_sink = _swa_tables(tbl_c, sinks)
    bf_lane = jnp.pad(b_f.astype(f32), ((0, 0), (0, LANES - N_HEADS)))[:, None, :]
    gf = ln_f.astype(f32)[None, :]

    x2 = x[0]
    for l in range(DEPTH):
        aqk, av, bqk, bv, c, fb, km = _inproj(x2, ln1[l][None, :], w1[l])
        km = km.reshape(s // MOBA_BLOCK, 512)
        qa, ka, va = _moba_prep(aqk, av, km)
        qb, kb, vb = _fox_prep(bqk, bv, fb, bf_lane[l])
        oa = _moba_flash(moba_far, qa, ka, va, moba_tiles)
        ob = _fox_flash(qb, kb, vb)
        oc = _swa(c, swa_bias, swa_sink[l])
        x2 = _mix(x2, ln1[l][None, :], wg[l], oa, ob, oc, wpa[l], wpb[l], wpc[l], wo[l])
        x2 = _ffn(x2, ln2[l][None, :], wup[l], conv_w[l], conv_b[l][None, :], wdn[l], gf,
                  final=(l == DEPTH - 1))
    return x2[None]
```

```python
import functools
import math

import numpy as np
import jax
import jax.numpy as jnp
from jax import lax
from jax.experimental import pallas as pl
from jax.experimental.pallas import tpu as pltpu

D_MODEL = 1024
DEPTH = 4
HEAD_DIM = 64
N_HEADS = 4
C_HEADS = 8
C_GROUP = 4
MOBA_BLOCK = 256
MOBA_TOPK = 3
Q_BLOCK = 128
WINDOW = 128
NUM_BUCKETS = 32
MAX_DISTANCE = 4096
D_FF = 2816
EPS = 1e-6
NEG = -1e30
SCALE = HEAD_DIM ** -0.5

LANES = 128
HALO = 8
TM = 512
TA = 1024
FF_CHUNK = 256
VMEM_LIMIT = 56 * 1024 * 1024

_A_QK = (0, 1024)
_A_V = (1024, 1536)
_B_QK = (1536, 2560)
_B_V = (2560, 3072)
_C_ALL = (3072, 4480)
_F_B = (4480, 4608)
_W1_COLS = 4608
_C_K = 1024
_C_V = 1152
_C_W = 1408

bf16 = jnp.bfloat16
f32 = jnp.float32
_HI = lax.Precision.HIGHEST


def _rms(x, g):
    ms = jnp.mean(x * x, axis=-1, keepdims=True)
    return x * lax.rsqrt(ms + EPS) * g


def _t5_bucket(dist):
    dist = jnp.maximum(dist, 0)
    max_exact = NUM_BUCKETS // 2
    d = jnp.maximum(dist.astype(jnp.float32), 1.0)
    large = max_exact + (jnp.log(d / max_exact) / math.log(MAX_DISTANCE / max_exact)
                         * (NUM_BUCKETS - max_exact)).astype(jnp.int32)
    large = jnp.minimum(large, NUM_BUCKETS - 1)
    return jnp.where(dist < max_exact, dist, large)


def _params(*sem):
    return pltpu.CompilerParams(dimension_semantics=sem, vmem_limit_bytes=VMEM_LIMIT)


def _resident(shape, index_map):
    return pl.BlockSpec(shape, index_map, pipeline_mode=pl.Buffered(1))


def _inproj_kernel(x_ref, g_ref, w_ref, aqk_ref, av_ref, bqk_ref, bv_ref, c_ref, fb_ref, km_ref):
    h = _rms(x_ref[...], g_ref[...]).astype(bf16)

    def mm(span):
        return jnp.dot(h, w_ref[:, span[0]:span[1]], preferred_element_type=f32)

    aqk = mm(_A_QK)
    aqk_ref[...] = aqk
    ka = aqk[:, 512:1024]
    km_ref[0] = jnp.mean(ka.reshape(TM // MOBA_BLOCK, MOBA_BLOCK, 512), axis=1)
    av_ref[...] = mm(_A_V).astype(bf16)
    bqk_ref[...] = mm(_B_QK)
    bv_ref[...] = mm(_B_V).astype(bf16)
    c_ref[...] = mm(_C_ALL).astype(bf16)
    fb_ref[...] = mm(_F_B)


def _inproj(x2, g, w1):
    s = x2.shape[0]
    nt = s // TM
    row = lambda w: pl.BlockSpec((TM, w), lambda i: (i, 0))
    return pl.pallas_call(
        _inproj_kernel,
        grid=(nt,),
        in_specs=[row(D_MODEL),
                  pl.BlockSpec((1, D_MODEL), lambda i: (0, 0)),
                  _resident((D_MODEL, _W1_COLS), lambda i: (0, 0))],
        out_specs=[row(1024), row(512), row(1024), row(512), row(_C_W), row(LANES),
                   pl.BlockSpec((1, TM // MOBA_BLOCK, 512), lambda i: (i, 0, 0))],
        out_shape=[jax.ShapeDtypeStruct((s, 1024), f32),
                   jax.ShapeDtypeStruct((s, 512), bf16),
                   jax.ShapeDtypeStruct((s, 1024), f32),
                   jax.ShapeDtypeStruct((s, 512), bf16),
                   jax.ShapeDtypeStruct((s, _C_W), bf16),
                   jax.ShapeDtypeStruct((s, LANES), f32),
                   jax.ShapeDtypeStruct((nt, TM // MOBA_BLOCK, 512), f32)],
        compiler_params=_params("arbitrary"),
        name="inproj",
    )(x2, g, w1)


def _moba_prep_kernel(aqk_ref, av_ref, km_ref, q_ref, k_ref, v_ref, *, nblk):
    i = pl.program_id(0)
    lane = lax.broadcasted_iota(jnp.int32, (TM, LANES), 1)
    kblk = (i * TM + lax.broadcasted_iota(jnp.int32, (TM, LANES), 0)) >> 8
    bidx = lax.broadcasted_iota(jnp.int32, (nblk, TM), 0)
    bidx_f = bidx.astype(f32)
    qblk = (i * TM + lax.broadcasted_iota(jnp.int32, (nblk, TM), 1)) >> 8
    for h in range(N_HEADS):
        q = aqk_ref[:, LANES * h:LANES * (h + 1)]
        km = km_ref[:, LANES * h:LANES * (h + 1)]
        gate = lax.dot_general(km, q, (((1,), (1,)), ((), ())), precision=_HI,
                               preferred_element_type=f32)
        gate = jnp.where(bidx < qblk, gate, -jnp.inf)
        sel = jnp.zeros((nblk, TM), f32)
        for _ in range(MOBA_TOPK):
            m = jnp.max(gate, axis=0, keepdims=True)
            first = jnp.min(jnp.where(gate == m, bidx_f, float(nblk)), axis=0, keepdims=True)
            hit = jnp.logical_and(bidx_f == first, m > -jnp.inf)
            sel = jnp.where(hit, 1.0, sel)
            gate = jnp.where(hit, -jnp.inf, gate)
        notsel = jnp.where(bidx == qblk, 0.0, 1.0 - sel)
        pad = jnp.zeros((LANES - nblk, TM), f32)
        if nblk == HEAD_DIM:
            ns = jnp.concatenate([pad, notsel], axis=0).T
        else:
            ns = jnp.concatenate([pad[:HEAD_DIM], notsel, pad[HEAD_DIM:]], axis=0).T
        q_ref[h] = (q * SCALE + ns).astype(bf16)
        k = aqk_ref[:, 512 + LANES * h:512 + LANES * (h + 1)]
        k_ref[h] = jnp.where(lane == HEAD_DIM + kblk, NEG, k).astype(bf16)
        v = av_ref[:, LANES * h:LANES * (h + 1)]
        v_ref[h] = jnp.where(lane == HEAD_DIM, 1.0, v.astype(f32)).astype(bf16)


def _moba_prep(aqk, av, km):
    s = aqk.shape[0]
    nblk = km.shape[0]
    hb = pl.BlockSpec((N_HEADS, TM, LANES), lambda i: (0, i, 0))
    out = jax.ShapeDtypeStruct((N_HEADS, s, LANES), bf16)
    return pl.pallas_call(
        functools.partial(_moba_prep_kernel, nblk=nblk),
        grid=(s // TM,),
        in_specs=[pl.BlockSpec((TM, 1024), lambda i: (i, 0)),
                  pl.BlockSpec((TM, 512), lambda i: (i, 0)),
                  pl.BlockSpec((nblk, 512), lambda i: (0, 0))],
        out_specs=[hb, hb, hb],
        out_shape=[out, out, out],
        compiler_params=_params("arbitrary"),
        name="moba_prep",
    )(aqk, av, km)


def _fox_prep_kernel(bqk_ref, bv_ref, fb_ref, bf_ref, q_ref, k_ref, v_ref, carry_ref):
    i = pl.program_id(0)

    @pl.when(i == 0)
    def _():
        carry_ref[...] = jnp.zeros_like(carry_ref)

    x = fb_ref[...] + bf_ref[...]
    logf = jnp.minimum(x, 0.0) - jnp.log1p(jnp.exp(-jnp.abs(x)))
    r = lax.broadcasted_iota(jnp.int32, (TM, TM), 0)
    c = lax.broadcasted_iota(jnp.int32, (TM, TM), 1)
    tri = jnp.where(c <= r, 1.0, 0.0).astype(f32)
    cum = jnp.dot(tri, logf, precision=_HI, preferred_element_type=f32) + carry_ref[...]
    carry_ref[...] = cum[TM - 1:TM, :]
    lane = lax.broadcasted_iota(jnp.int32, (TM, LANES), 1)
    d = HEAD_DIM
    for h in range(N_HEADS):
        cb = jnp.broadcast_to(cum[:, h:h + 1], (TM, LANES))
        hi = cb.astype(bf16).astype(f32)
        r1 = cb - hi
        mid = r1.astype(bf16).astype(f32)
        lo = r1 - mid
        q = bqk_ref[:, LANES * h:LANES * (h + 1)] * SCALE
        qa = jnp.where(lane < d, q,
             jnp.where(lane == d, hi,
             jnp.where(lane == d + 1, mid,
             jnp.where(lane == d + 2, lo,
             jnp.where(lane < d + 6, 1.0, 0.0)))))
        q_ref[h] = qa.astype(bf16)
        k = bqk_ref[:, 512 + LANES * h:512 + LANES * (h + 1)]
        ka = jnp.where(lane < d, k,
             jnp.where(lane < d + 3, 1.0,
             jnp.where(lane == d + 3, -hi,
             jnp.where(lane == d + 4, -mid,
             jnp.where(lane == d + 5, -lo, 0.0)))))
        k_ref[h] = ka.astype(bf16)
        v = bv_ref[:, LANES * h:LANES * (h + 1)]
        v_ref[h] = jnp.where(lane == d, 1.0, v.astype(f32)).astype(bf16)


def _fox_prep(bqk, bv, fb, bf_lane):
    s = bqk.shape[0]
    hb = pl.BlockSpec((N_HEADS, TM, LANES), lambda i: (0, i, 0))
    out = jax.ShapeDtypeStruct((N_HEADS, s, LANES), bf16)
    return pl.pallas_call(
        _fox_prep_kernel,
        grid=(s // TM,),
        in_specs=[pl.BlockSpec((TM, 1024), lambda i: (i, 0)),
                  pl.BlockSpec((TM, 512), lambda i: (i, 0)),
                  pl.BlockSpec((TM, LANES), lambda i: (i, 0)),
                  pl.BlockSpec((1, LANES), lambda i: (0, 0))],
        out_specs=[hb, hb, hb],
        out_shape=[out, out, out],
        scratch_shapes=[pltpu.VMEM((1, LANES), f32)],
        compiler_params=_params("arbitrary"),
        name="fox_prep",
    )(bqk, bv, fb, bf_lane)


def _flash_step(q, k_ref, v_ref, j, carry, add_bias):
    m, acc = carry
    off = pl.multiple_of(j * TA, TA)
    k = k_ref[0, pl.ds(off, TA), :]
    v = v_ref[0, pl.ds(off, TA), :]
    s = lax.dot_general(q, k, (((1,), (1,)), ((), ())), preferred_element_type=f32)
    s = add_bias(s)
    m_new = jnp.maximum(m, jnp.max(s, axis=-1, keepdims=True))
    alpha = jnp.exp(m - m_new)
    p = jnp.exp(s - m_new)
    acc = alpha * acc + jnp.dot(p.astype(bf16), v, preferred_element_type=f32)
    return m_new, acc


def _flash_finish(o_ref, acc):
    o_ref[0] = (acc / acc[:, HEAD_DIM:HEAD_DIM + 1]).astype(o_ref.dtype)


def _flash_init():
    return jnp.full((TA, 1), -jnp.inf, f32), jnp.zeros((TA, LANES), f32)


def _fox_flash_kernel(q_ref, k_ref, v_ref, o_ref):
    i = pl.program_id(1)
    q = q_ref[0]
    carry = lax.fori_loop(
        0, i, lambda j, c: _flash_step(q, k_ref, v_ref, j, c, lambda s: s), _flash_init())
    row = lax.broadcasted_iota(jnp.int32, (TA, TA), 0)
    col = lax.broadcasted_iota(jnp.int32, (TA, TA), 1)
    _, acc = _flash_step(q, k_ref, v_ref, i, carry, lambda s: jnp.where(col <= row, s, NEG))
    _flash_finish(o_ref, acc)


def _moba_flash_kernel(far_ref, q_ref, k_ref, v_ref, bias_ref, o_ref, *, near):
    h = pl.program_id(0)
    i = pl.program_id(1)
    q = q_ref[0]
    far = far_ref[h]
    n_far = jnp.maximum(i - (near - 1), 0)
    carry = lax.fori_loop(
        0, n_far, lambda j, c: _flash_step(q, k_ref, v_ref, j, c, lambda s: s + far), _flash_init())
    carry = lax.fori_loop(
        n_far, i + 1,
        lambda j, c: _flash_step(q, k_ref, v_ref, j, c, lambda s: s + bias_ref[0, i - j]), carry)
    _flash_finish(o_ref, carry[1])


def _flash_specs(s):
    qspec = pl.BlockSpec((1, TA, LANES), lambda h, i: (h, i, 0))
    kvspec = _resident((1, s, LANES), lambda h, i: (h, 0, 0))
    return qspec, kvspec


def _fox_flash(q, k, v):
    s = q.shape[1]
    qspec, kvspec = _flash_specs(s)
    return pl.pallas_call(
        _fox_flash_kernel,
        grid=(N_HEADS, s // TA),
        in_specs=[qspec, kvspec, kvspec],
        out_specs=qspec,
        out_shape=jax.ShapeDtypeStruct((N_HEADS, s, LANES), bf16),
        compiler_params=_params("arbitrary", "arbitrary"),
        name="fox_flash",
    )(q, k, v)


def _moba_flash(far, q, k, v, bias):
    s = q.shape[1]
    near = bias.shape[1]
    qspec, kvspec = _flash_specs(s)
    return pl.pallas_call(
        functools.partial(_moba_flash_kernel, near=near),
        grid=(N_HEADS, s // TA),
        in_specs=[pl.BlockSpec(memory_space=pltpu.SMEM), qspec, kvspec, kvspec,
                  _resident((1, near, TA, TA), lambda h, i: (h, 0, 0, 0))],
        out_specs=qspec,
        out_shape=jax.ShapeDtypeStruct((N_HEADS, s, LANES), bf16),
        compiler_params=_params("arbitrary", "arbitrary"),
        name="moba_flash",
    )(far, q, k, v, bias)


def _swa_kernel(cur_ref, prev_ref, bias_ref, sink_ref, o_ref):
    i = pl.program_id(0)
    first = jnp.where(i == 0, 1, 0)
    lane = lax.broadcasted_iota(jnp.int32, (Q_BLOCK, LANES), 1)
    qb = Q_BLOCK
    for b in range(TM // qb):
        if b == 0:
            kband = jnp.concatenate([prev_ref[TM - qb:TM, _C_K:_C_V], cur_ref[0:qb, _C_K:_C_V]], axis=0)
            vband = jnp.concatenate([prev_ref[TM - qb:TM, _C_V:_C_W], cur_ref[0:qb, _C_V:_C_W]], axis=0)
            variant = first
        else:
            kband = cur_ref[qb * (b - 1):qb * (b + 1), _C_K:_C_V]
            vband = cur_ref[qb * (b - 1):qb * (b + 1), _C_V:_C_W]
            variant = 0
        outs = []
        for g in range(C_HEADS // C_GROUP):
            qs = jnp.concatenate(
                [cur_ref[qb * b:qb * (b + 1), LANES * (C_GROUP * g + u):LANES * (C_GROUP * g + u + 1)]
                 for u in range(C_GROUP)], axis=0)
            s = lax.dot_general(qs, kband, (((1,), (1,)), ((), ())), preferred_element_type=f32)
            s = s + bias_ref[variant, g]
            sink = sink_ref[g]
            m = jnp.maximum(jnp.max(s, axis=-1, keepdims=True), sink)
            p = jnp.exp(s - m)
            den = jnp.sum(p, axis=-1, keepdims=True) + jnp.exp(sink - m)
            pb = p.astype(bf16)
            for u in range(C_GROUP):
                half = (C_GROUP * g + u) % 2
                vg = vband[:, 0:LANES] if g == half else vband[:, LANES:2 * LANES]
                pv = jnp.dot(pb[qb * u:qb * (u + 1)], vg, preferred_element_type=f32)
                outs.append(pv / den[qb * u:qb * (u + 1)])
        for pr in range(C_HEADS // 2):
            o_ref[qb * b:qb * (b + 1), LANES * pr:LANES * (pr + 1)] = jnp.where(
                lane < HEAD_DIM, outs[2 * pr], outs[2 * pr + 1]).astype(o_ref.dtype)


def _swa(c, bias, sink):
    s = c.shape[0]
    return pl.pallas_call(
        _swa_kernel,
        grid=(s // TM,),
        in_specs=[pl.BlockSpec((TM, _C_W), lambda i: (i, 0)),
                  pl.BlockSpec((TM, _C_W), lambda i: (jnp.maximum(i - 1, 0), 0)),
                  pl.BlockSpec(bias.shape, lambda i: (0, 0, 0, 0)),
                  pl.BlockSpec(sink.shape, lambda i: (0, 0, 0))],
        out_specs=pl.BlockSpec((TM, 512), lambda i: (i, 0)),
        out_shape=jax.ShapeDtypeStruct((s, 512), bf16),
        compiler_params=_params("arbitrary"),
        name="swa",
    )(c, c, bias, sink)


def _mix_kernel(x_ref, g_ref, wg_ref, oa_ref, ob_ref, oc_ref, wpa_ref, wpb_ref, wpc_ref, wo_ref,
                out_ref, mg_ref):
    x = x_ref[...]
    h = _rms(x, g_ref[...]).astype(bf16)
    oc = oc_ref[...]
    cw = 512
    for n in range(D_MODEL // cw):
        lo = cw * n

        def heads(o_ref, w_ref):
            acc = jnp.dot(o_ref[0], w_ref[0:LANES, lo:lo + cw], preferred_element_type=f32)
            for hh in range(1, N_HEADS):
                acc += jnp.dot(o_ref[hh], w_ref[LANES * hh:LANES * (hh + 1), lo:lo + cw],
                               preferred_element_type=f32)
            return acc

        def gate(which):
            z = jnp.dot(h, wg_ref[:, D_MODEL * which + lo:D_MODEL * which + lo + cw],
                        preferred_element_type=f32)
            return jax.nn.sigmoid(z)

        mg = gate(0) * heads(oa_ref, wpa_ref)
        mg += gate(1) * heads(ob_ref, wpb_ref)
        mg += gate(2) * jnp.dot(oc, wpc_ref[:, lo:lo + cw], preferred_element_type=f32)
        mg_ref[:, lo:lo + cw] = mg.astype(bf16)
    out_ref[...] = x + jnp.dot(mg_ref[...], wo_ref[...], preferred_element_type=f32)


def _mix(x2, g, wg, oa, ob, oc, wpa, wpb, wpc, wo):
    s = x2.shape[0]
    hb = pl.BlockSpec((N_HEADS, TM, LANES), lambda i: (0, i, 0))
    full = lambda a: _resident(a.shape, lambda i: (0,) * a.ndim)
    return pl.pallas_call(
        _mix_kernel,
        grid=(s // TM,),
        in_specs=[pl.BlockSpec((TM, D_MODEL), lambda i: (i, 0)),
                  pl.BlockSpec((1, D_MODEL), lambda i: (0, 0)),
                  full(wg), hb, hb,
                  pl.BlockSpec((TM, 512), lambda i: (i, 0)),
                  full(wpa), full(wpb), full(wpc), full(wo)],
        out_specs=pl.BlockSpec((TM, D_MODEL), lambda i: (i, 0)),
        out_shape=jax.ShapeDtypeStruct((s, D_MODEL), f32),
        scratch_shapes=[pltpu.VMEM((TM, D_MODEL), bf16)],
        compiler_params=_params("arbitrary"),
        name="mix",
    )(x2, g, wg, oa, ob, oc, wpa, wpb, wpc, wo)


def _ffn_kernel(x_ref, xh_ref, g_ref, wup_ref, cw_ref, cb_ref, wdn_ref, gf_ref, out_ref,
                u_ref, act_ref, *, final):
    i = pl.program_id(0)
    x = x_ref[...]
    halo = jnp.where(i > 0, xh_ref[...], 0.0)
    h2 = _rms(jnp.concatenate([halo, x], axis=0), g_ref[...]).astype(bf16)

    def conv(lo):
        u_ref[...] = jnp.dot(h2, wup_ref[:, lo:lo + FF_CHUNK], preferred_element_type=f32)
        w = cw_ref[:, lo:lo + FF_CHUNK]
        return (w[2:3] * u_ref[HALO:HALO + TM] + w[1:2] * u_ref[HALO - 1:HALO - 1 + TM]
                + w[0:1] * u_ref[HALO - 2:HALO - 2 + TM] + cb_ref[:, lo:lo + FF_CHUNK])

    for c in range(D_FF // FF_CHUNK):
        a = conv(FF_CHUNK * c)
        b = conv(D_FF + FF_CHUNK * c)
        act_ref[:, FF_CHUNK * c:FF_CHUNK * (c + 1)] = (a * jax.nn.sigmoid(a) * b).astype(bf16)
    o = x + jnp.dot(act_ref[...], wdn_ref[...], preferred_element_type=f32)
    if final:
        o = _rms(o, gf_ref[...])
    out_ref[...] = o


def _ffn(x2, g, wup, cw, cb, wdn, gf, final):
    s = x2.shape[0]
    full = lambda a: _resident(a.shape, lambda i: (0,) * a.ndim)
    vec = lambda a: pl.BlockSpec(a.shape, lambda i: (0,) * a.ndim)
    return pl.pallas_call(
        functools.partial(_ffn_kernel, final=final),
        grid=(s // TM,),
        in_specs=[pl.BlockSpec((TM, D_MODEL), lambda i: (i, 0)),
                  pl.BlockSpec((HALO, D_MODEL), lambda i: (jnp.maximum(i * (TM // HALO) - 1, 0), 0)),
                  vec(g), full(wup), vec(cw), vec(cb), full(wdn), vec(gf)],
        out_specs=pl.BlockSpec((TM, D_MODEL), lambda i: (i, 0)),
        out_shape=jax.ShapeDtypeStruct((s, D_MODEL), f32),
        scratch_shapes=[pltpu.VMEM((TM + HALO, FF_CHUNK), f32), pltpu.VMEM((TM, D_FF), bf16)],
        compiler_params=_params("arbitrary"),
        name="ffn",
    )(x2, x2, g, wup, cw, cb, wdn, gf)


def _pad_heads(w, nh):
    l, d, _ = w.shape
    w = jnp.pad(w.reshape(l, d, nh, HEAD_DIM), ((0, 0), (0, 0), (0, 0), (0, LANES - HEAD_DIM)))
    return w.reshape(l, d, nh * LANES)


def _pad_head_rows(w, nh):
    l, _, d = w.shape
    w = jnp.pad(w.reshape(l, nh, HEAD_DIM, d), ((0, 0), (0, 0), (0, LANES - HEAD_DIM), (0, 0)))
    return w.reshape(l, nh * LANES, d)


def _fused_in_weight(w_in):
    l = w_in.shape[0]
    cuts = np.cumsum([0, 256, 256, 256, 256, 256, 256, N_HEADS, 512, 128, 128]).tolist()
    qa, ka, va, qb, kb, vb, fb, qc, kc, vc = (w_in[..., cuts[n]:cuts[n + 1]] for n in range(10))
    qc = (qc * SCALE).reshape(l, D_MODEL, C_HEADS, HEAD_DIM)
    zero = jnp.zeros_like(qc[:, :, 0])
    qc_pad = jnp.concatenate(
        [jnp.concatenate([qc[:, :, hq], zero] if hq < C_GROUP else [zero, qc[:, :, hq]], axis=-1)
         for hq in range(C_HEADS)], axis=-1)
    vc2 = jnp.concatenate([vc, vc[..., HEAD_DIM:], vc[..., :HEAD_DIM]], axis=-1)
    fbp = jnp.pad(fb, ((0, 0), (0, 0), (0, LANES - N_HEADS)))
    w1 = jnp.concatenate([_pad_heads(qa, N_HEADS), _pad_heads(ka, N_HEADS), _pad_heads(va, N_HEADS),
                          _pad_heads(qb, N_HEADS), _pad_heads(kb, N_HEADS), _pad_heads(vb, N_HEADS),
                          qc_pad, kc, vc2, fbp], axis=-1)
    assert w1.shape[-1] == _W1_COLS
    return w1.astype(bf16), w_in[..., cuts[10]:].astype(bf16)


def _bias_lookup(tbl, dist):
    bucket = _t5_bucket(jnp.asarray(dist, jnp.int32))
    out = jnp.zeros((tbl.shape[0],) + bucket.shape, f32)
    for b in range(NUM_BUCKETS):
        col = tbl[:, b].astype(f32).reshape((-1,) + (1,) * bucket.ndim)
        out = jnp.where(bucket[None] == b, col, out)
    return out


def _toeplitz_kernel(u_ref, o_ref):
    x = jnp.broadcast_to(u_ref[0, 0], (TA, 2 * TA))
    o_ref[0, 0] = pltpu.roll(x, 0, 1, stride=1, stride_axis=0)[:, :TA]


def _moba_bias_tiles(tbl_a, s):
    near = min(-(-(MAX_DISTANCE + TA - 1) // TA), s // TA)
    m = np.arange(2 * TA)
    rel = np.where(m < TA, -m, np.where(m == TA, 0, 2 * TA - m))
    dist = np.arange(near)[:, None] * TA + rel[None, :]
    gen = jnp.where(dist >= 0, _bias_lookup(tbl_a, np.maximum(dist, 0)), NEG)
    tiles = pl.pallas_call(
        _toeplitz_kernel,
        grid=(N_HEADS, near),
        in_specs=[pl.BlockSpec((1, 1, 1, 2 * TA), lambda h, d: (h, d, 0, 0))],
        out_specs=pl.BlockSpec((1, 1, TA, TA), lambda h, d: (h, d, 0, 0)),
        out_shape=jax.ShapeDtypeStruct((N_HEADS, near, TA, TA), f32),
        compiler_params=_params("arbitrary", "arbitrary"),
        name="moba_bias_tiles",
    )(gen[:, :, None, :])
    far = _bias_lookup(tbl_a, np.asarray([near * TA]))[:, 0]
    return tiles, far


def _swa_tables(tbl_c, sinks):
    tl = np.arange(Q_BLOCK)
    sl = np.arange(2 * Q_BLOCK)
    dist = tl[:, None] + Q_BLOCK - sl[None, :]
    bias = _bias_lookup(tbl_c, dist)
    band = (dist >= 0) & (dist < WINDOW)
    masks = np.stack([band, band & (sl[None, :] >= Q_BLOCK)])
    table = jnp.where(masks[:, None], bias[None], NEG)
    table = table.reshape(2, C_HEADS // C_GROUP, C_GROUP * Q_BLOCK, 2 * Q_BLOCK).astype(f32)
    sink = jnp.repeat(sinks.astype(f32), Q_BLOCK, axis=-1)
    sink = sink.reshape(sinks.shape[0], C_HEADS // C_GROUP, C_GROUP * Q_BLOCK, 1)
    return table, sink


def kernel(x, ln1, w_in, b_f, sinks, w_pa, w_pb, w_pc, w_o, ln2, w_up, conv_w, conv_b, w_down,
           rel_bias, ln_f):
    bsz, s, _ = x.shape
    assert bsz == 1 and s % TM == 0 and s % TA == 0 and TM % MOBA_BLOCK == 0
    assert s // MOBA_BLOCK <= HEAD_DIM
    w1, wg = _fused_in_weight(w_in)
    wpa = _pad_head_rows(w_pa, N_HEADS).astype(bf16)
    wpb = _pad_head_rows(w_pb, N_HEADS).astype(bf16)
    wpc, wo, wup, wdn = (w.astype(bf16) for w in (w_pc, w_o, w_up, w_down))
    tbl_a = rel_bias[:, :N_HEADS].T
    tbl_c = rel_bias[:, N_HEADS:].T
    moba_tiles, moba_far = _moba_bias_tiles(tbl_a, s)
    swa_bias, swa_sink = _swa_tables(tbl_c, sinks)
    bf_lane = jnp.pad(b_f.astype(f32), ((0, 0), (0, LANES - N_HEADS)))[:, None, :]
    gf = ln_f.astype(f32)[None, :]

    x2 = x[0]
    for l in range(DEPTH):
        aqk, av, bqk, bv, c, fb, km = _inproj(x2, ln1[l][None, :], w1[l])
        km = km.reshape(s // MOBA_BLOCK, 512)
        qa, ka, va = _moba_prep(aqk, av, km)
        qb, kb, vb = _fox_prep(bqk, bv, fb, bf_lane[l])
        oa = _moba_flash(moba_far, qa, ka, va, moba_tiles)
        ob = _fox_flash(qb, kb, vb)
        oc = _swa(c, swa_bias, swa_sink[l])
        x2 = _mix(x2, ln1[l][None, :], wg[l], oa, ob, oc, wpa[l], wpb[l], wpc[l], wo[l])
        x2 = _ffn(x2, ln2[l][None, :], wup[l], conv_w[l], conv_b[l][None, :], wdn[l], gf,
                  final=(l == DEPTH - 1))
    return x2[None]
```

```python
import functools
import math

import numpy as np
import jax
import jax.numpy as jnp
from jax import lax
from jax.experimental import pallas as pl
from jax.experimental.pallas import tpu as pltpu

D_MODEL = 1024
DEPTH = 4
HEAD_DIM = 64
N_HEADS = 4
C_HEADS = 8
C_GROUP = 4
MOBA_BLOCK = 256
MOBA_TOPK = 3
Q_BLOCK = 128
WINDOW = 128
NUM_BUCKETS = 32
MAX_DISTANCE = 4096
D_FF = 2816
EPS = 1e-6
NEG = -1e30
SCALE = HEAD_DIM ** -0.5

LANES = 128
HALO = 8
TM = 512
TA = 1024
FF_CHUNK = 256
VMEM_LIMIT = 56 * 1024 * 1024

_QKV_A = (0, 768)
_QKV_B = (768, 1536)
_C_ALL = (1536, 2944)
_F_B = (2944, 3072)
_W1_COLS = 3072
_C_K = 1024
_C_V = 1152
_C_W = 1408

bf16 = jnp.bfloat16
f32 = jnp.float32
_HI = lax.Precision.HIGHEST


def _rms(x, g):
    ms = jnp.mean(x * x, axis=-1, keepdims=True)
    return x * lax.rsqrt(ms + EPS) * g


def _t5_bucket(dist):
    dist = jnp.maximum(dist, 0)
    max_exact = NUM_BUCKETS // 2
    d = jnp.maximum(dist.astype(jnp.float32), 1.0)
    large = max_exact + (jnp.log(d / max_exact) / math.log(MAX_DISTANCE / max_exact)
                         * (NUM_BUCKETS - max_exact)).astype(jnp.int32)
    large = jnp.minimum(large, NUM_BUCKETS - 1)
    return jnp.where(dist < max_exact, dist, large)


def _params(*sem):
    return pltpu.CompilerParams(dimension_semantics=sem, vmem_limit_bytes=VMEM_LIMIT)


def _resident(shape, index_map):
    return pl.BlockSpec(shape, index_map, pipeline_mode=pl.Buffered(1))


def _head_lanes(z, h, part):
    lo = N_HEADS * HEAD_DIM * part + LANES * (h // 2)
    col = z[:, lo:lo + LANES]
    return pltpu.roll(col, HEAD_DIM, 1) if h % 2 else col


def _moba_operands(i, za, km_ref, q_ref, k_ref, v_ref, nblk):
    width = N_HEADS * HEAD_DIM
    lane = lax.broadcasted_iota(jnp.int32, (TM, LANES), 1)
    kblk = (i * TM + lax.broadcasted_iota(jnp.int32, (TM, LANES), 0)) >> 8
    bidx = lax.broadcasted_iota(jnp.int32, (nblk, TM), 0)
    bidx_f = bidx.astype(f32)
    qblk = (i * TM + lax.broadcasted_iota(jnp.int32, (nblk, TM), 1)) >> 8
    member = jnp.where(bidx == qblk, 1.0 / MOBA_BLOCK, 0.0).astype(f32)
    km_ref[...] += jnp.dot(member, za[:, width:2 * width], precision=_HI, preferred_element_type=f32)
    km = km_ref[...]
    km_head = lax.broadcasted_iota(jnp.int32, (nblk, width), 1) >> 6
    q_all = za[:, 0:width]
    for h in range(N_HEADS):
        gate = lax.dot_general(jnp.where(km_head == h, km, 0.0), q_all, (((1,), (1,)), ((), ())),
                               precision=_HI, preferred_element_type=f32)
        gate = jnp.where(bidx < qblk, gate, -jnp.inf)
        sel = jnp.zeros((nblk, TM), f32)
        for _ in range(MOBA_TOPK):
            m = jnp.max(gate, axis=0, keepdims=True)
            first = jnp.min(jnp.where(gate == m, bidx_f, float(nblk)), axis=0, keepdims=True)
            hit = jnp.logical_and(bidx_f == first, m > -jnp.inf)
            sel = jnp.where(hit, 1.0, sel)
            gate = jnp.where(hit, -jnp.inf, gate)
        notsel = jnp.where(bidx == qblk, 0.0, 1.0 - sel)
        pad = jnp.zeros((LANES - nblk, TM), f32)
        if nblk == HEAD_DIM:
            ns = jnp.concatenate([pad, notsel], axis=0).T
        else:
            ns = jnp.concatenate([pad[:HEAD_DIM], notsel, pad[HEAD_DIM:]], axis=0).T
        q_ref[h] = jnp.where(lane < HEAD_DIM, _head_lanes(za, h, 0) * SCALE, ns).astype(bf16)
        k_ref[h] = jnp.where(lane < HEAD_DIM, _head_lanes(za, h, 1),
                             jnp.where(lane == HEAD_DIM + kblk, NEG, 0.0)).astype(bf16)
        v_ref[h] = jnp.where(lane < HEAD_DIM, _head_lanes(za, h, 2),
                             jnp.where(lane == HEAD_DIM, 1.0, 0.0)).astype(bf16)


def _fox_operands(zb, fb, cum_ref, q_ref, k_ref, v_ref):
    logf = jnp.minimum(fb, 0.0) - jnp.log1p(jnp.exp(-jnp.abs(fb)))
    r = lax.broadcasted_iota(jnp.int32, (TM, TM), 0)
    c = lax.broadcasted_iota(jnp.int32, (TM, TM), 1)
    tri = jnp.where(c <= r, 1.0, 0.0).astype(f32)
    cum = jnp.dot(tri, logf, precision=_HI, preferred_element_type=f32) + cum_ref[...]
    cum_ref[...] = cum[TM - 1:TM, :]
    lane = lax.broadcasted_iota(jnp.int32, (TM, LANES), 1)
    d = HEAD_DIM
    for h in range(N_HEADS):
        cb = jnp.broadcast_to(cum[:, h:h + 1], (TM, LANES))
        hi = cb.astype(bf16).astype(f32)
        r1 = cb - hi
        mid = r1.astype(bf16).astype(f32)
        lo = r1 - mid
        qa = jnp.where(lane < d, _head_lanes(zb, h, 0) * SCALE,
             jnp.where(lane == d, hi,
             jnp.where(lane == d + 1, mid,
             jnp.where(lane == d + 2, lo,
             jnp.where(lane < d + 6, 1.0, 0.0)))))
        q_ref[h] = qa.astype(bf16)
        ka = jnp.where(lane < d, _head_lanes(zb, h, 1),
             jnp.where(lane < d + 3, 1.0,
             jnp.where(lane == d + 3, -hi,
             jnp.where(lane == d + 4, -mid,
             jnp.where(lane == d + 5, -lo, 0.0)))))
        k_ref[h] = ka.astype(bf16)
        v_ref[h] = jnp.where(lane < d, _head_lanes(zb, h, 2),
                             jnp.where(lane == d, 1.0, 0.0)).astype(bf16)


def _inproj_kernel(x_ref, g_ref, w_ref, bf_ref, qa_ref, ka_ref, va_ref, qb_ref, kb_ref, vb_ref, c_ref,
                   km_ref, cum_ref, *, nblk):
    i = pl.program_id(0)

    @pl.when(i == 0)
    def _():
        km_ref[...] = jnp.zeros_like(km_ref)
        cum_ref[...] = jnp.zeros_like(cum_ref)

    h = _rms(x_ref[...], g_ref[...]).astype(bf16)

    def mm(span):
        return jnp.dot(h, w_ref[:, span[0]:span[1]], preferred_element_type=f32)

    _moba_operands(i, mm(_QKV_A), km_ref, qa_ref, ka_ref, va_ref, nblk)
    _fox_operands(mm(_QKV_B), mm(_F_B) + bf_ref[...], cum_ref, qb_ref, kb_ref, vb_ref)
    c_ref[...] = mm(_C_ALL).astype(bf16)


def _inproj(x2, g, w1, bf_lane):
    s = x2.shape[0]
    nblk = s // MOBA_BLOCK
    hb = pl.BlockSpec((N_HEADS, TM, LANES), lambda i: (0, i, 0))
    aug = jax.ShapeDtypeStruct((N_HEADS, s, LANES), bf16)
    return pl.pallas_call(
        functools.partial(_inproj_kernel, nblk=nblk),
        grid=(s // TM,),
        in_specs=[pl.BlockSpec((TM, D_MODEL), lambda i: (i, 0)),
                  pl.BlockSpec((1, D_MODEL), lambda i: (0, 0)),
                  _resident((D_MODEL, _W1_COLS), lambda i: (0, 0)),
                  pl.BlockSpec((1, LANES), lambda i: (0, 0))],
        out_specs=[hb] * 6 + [pl.BlockSpec((TM, _C_W), lambda i: (i, 0))],
        out_shape=[aug] * 6 + [jax.ShapeDtypeStruct((s, _C_W), bf16)],
        scratch_shapes=[pltpu.VMEM((nblk, N_HEADS * HEAD_DIM), f32), pltpu.VMEM((1, LANES), f32)],
        compiler_params=_params("arbitrary"),
        name="inproj",
    )(x2, g, w1, bf_lane)


def _flash_step(q, k_ref, v_ref, j, carry, add_bias):
    m, acc = carry
    off = pl.multiple_of(j * TA, TA)
    k = k_ref[0, pl.ds(off, TA), :]
    v = v_ref[0, pl.ds(off, TA), :]
    s = lax.dot_general(q, k, (((1,), (1,)), ((), ())), preferred_element_type=f32)
    s = add_bias(s)
    m_new = jnp.maximum(m, jnp.max(s, axis=-1, keepdims=True))
    alpha = jnp.exp(m - m_new)
    p = jnp.exp(s - m_new)
    acc = alpha * acc + jnp.dot(p.astype(bf16), v, preferred_element_type=f32)
    return m_new, acc


def _flash_finish(o_ref, acc):
    o_ref[0] = (acc / acc[:, HEAD_DIM:HEAD_DIM + 1]).astype(o_ref.dtype)


def _flash_init():
    return jnp.full((TA, 1), -jnp.inf, f32), jnp.zeros((TA, LANES), f32)


def _fox_flash_kernel(q_ref, k_ref, v_ref, o_ref):
    i = pl.program_id(1)
    q = q_ref[0]
    carry = lax.fori_loop(
        0, i, lambda j, c: _flash_step(q, k_ref, v_ref, j, c, lambda s: s), _flash_init())
    row = lax.broadcasted_iota(jnp.int32, (TA, TA), 0)
    col = lax.broadcasted_iota(jnp.int32, (TA, TA), 1)
    _, acc = _flash_step(q, k_ref, v_ref, i, carry, lambda s: jnp.where(col <= row, s, NEG))
    _flash_finish(o_ref, acc)


def _moba_flash_kernel(far_ref, q_ref, k_ref, v_ref, bias_ref, o_ref, *, near):
    h = pl.program_id(0)
    i = pl.program_id(1)
    q = q_ref[0]
    far = far_ref[h]
    n_far = jnp.maximum(i - (near - 1), 0)
    carry = lax.fori_loop(
        0, n_far, lambda j, c: _flash_step(q, k_ref, v_ref, j, c, lambda s: s + far), _flash_init())
    carry = lax.fori_loop(
        n_far, i + 1,
        lambda j, c: _flash_step(q, k_ref, v_ref, j, c, lambda s: s + bias_ref[0, i - j]), carry)
    _flash_finish(o_ref, carry[1])


def _flash_specs(s):
    qspec = pl.BlockSpec((1, TA, LANES), lambda h, i: (h, i, 0))
    kvspec = _resident((1, s, LANES), lambda h, i: (h, 0, 0))
    return qspec, kvspec


def _fox_flash(q, k, v):
    s = q.shape[1]
    qspec, kvspec = _flash_specs(s)
    return pl.pallas_call(
        _fox_flash_kernel,
        grid=(N_HEADS, s // TA),
        in_specs=[qspec, kvspec, kvspec],
        out_specs=qspec,
        out_shape=jax.ShapeDtypeStruct((N_HEADS, s, LANES), bf16),
        compiler_params=_params("arbitrary", "arbitrary"),
        name="fox_flash",
    )(q, k, v)


def _moba_flash(far, q, k, v, bias):
    s = q.shape[1]
    near = bias.shape[1]
    qspec, kvspec = _flash_specs(s)
    return pl.pallas_call(
        functools.partial(_moba_flash_kernel, near=near),
        grid=(N_HEADS, s // TA),
        in_specs=[pl.BlockSpec(memory_space=pltpu.SMEM), qspec, kvspec, kvspec,
                  _resident((1, near, TA, TA), lambda h, i: (h, 0, 0, 0))],
        out_specs=qspec,
        out_shape=jax.ShapeDtypeStruct((N_HEADS, s, LANES), bf16),
        compiler_params=_params("arbitrary", "arbitrary"),
        name="moba_flash",
    )(far, q, k, v, bias)


def _swa_kernel(cur_ref, prev_ref, bias_ref, sink_ref, o_ref):
    i = pl.program_id(0)
    first = jnp.where(i == 0, 1, 0)
    lane = lax.broadcasted_iota(jnp.int32, (Q_BLOCK, LANES), 1)
    qb = Q_BLOCK
    for b in range(TM // qb):
        if b == 0:
            kband = jnp.concatenate([prev_ref[TM - qb:TM, _C_K:_C_V], cur_ref[0:qb, _C_K:_C_V]], axis=0)
            vband = jnp.concatenate([prev_ref[TM - qb:TM, _C_V:_C_W], cur_ref[0:qb, _C_V:_C_W]], axis=0)
            variant = first
        else:
            kband = cur_ref[qb * (b - 1):qb * (b + 1), _C_K:_C_V]
            vband = cur_ref[qb * (b - 1):qb * (b + 1), _C_V:_C_W]
            variant = 0
        outs = []
        for g in range(C_HEADS // C_GROUP):
            qs = jnp.concatenate(
                [cur_ref[qb * b:qb * (b + 1), LANES * (C_GROUP * g + u):LANES * (C_GROUP * g + u + 1)]
                 for u in range(C_GROUP)], axis=0)
            s = lax.dot_general(qs, kband, (((1,), (1,)), ((), ())), preferred_element_type=f32)
            s = s + bias_ref[variant, g]
            sink = sink_ref[g]
            m = jnp.maximum(jnp.max(s, axis=-1, keepdims=True), sink)
            p = jnp.exp(s - m)
            den = jnp.sum(p, axis=-1, keepdims=True) + jnp.exp(sink - m)
            pb = p.astype(bf16)
            for u in range(C_GROUP):
                half = (C_GROUP * g + u) % 2
                vg = vband[:, 0:LANES] if g == half else vband[:, LANES:2 * LANES]
                pv = jnp.dot(pb[qb * u:qb * (u + 1)], vg, preferred_element_type=f32)
                outs.append(pv / den[qb * u:qb * (u + 1)])
        for pr in range(C_HEADS // 2):
            o_ref[qb * b:qb * (b + 1), LANES * pr:LANES * (pr + 1)] = jnp.where(
                lane < HEAD_DIM, outs[2 * pr], outs[2 * pr + 1]).astype(o_ref.dtype)


def _swa(c, bias, sink):
    s = c.shape[0]
    return pl.pallas_call(
        _swa_kernel,
        grid=(s // TM,),
        in_specs=[pl.BlockSpec((TM, _C_W), lambda i: (i, 0)),
                  pl.BlockSpec((TM, _C_W), lambda i: (jnp.maximum(i - 1, 0), 0)),
                  pl.BlockSpec(bias.shape, lambda i: (0, 0, 0, 0)),
                  pl.BlockSpec(sink.shape, lambda i: (0, 0, 0))],
        out_specs=pl.BlockSpec((TM, 512), lambda i: (i, 0)),
        out_shape=jax.ShapeDtypeStruct((s, 512), bf16),
        compiler_params=_params("arbitrary"),
        name="swa",
    )(c, c, bias, sink)


def _mix_ffn_kernel(x_ref, g1_ref, wg_ref, oa_ref, ob_ref, oc_ref, wpa_ref, wpb_ref, wpc_ref, wo_ref,
                    g2_ref, wup_ref, cw_ref, cb_ref, wdn_ref, gf_ref, out_ref,
                    mg_ref, halo_ref, u_ref, act_ref, *, final):
    @pl.when(pl.program_id(0) == 0)
    def _():
        halo_ref[...] = jnp.zeros_like(halo_ref)

    x = x_ref[...]
    h = _rms(x, g1_ref[...]).astype(bf16)
    oc = oc_ref[...]
    cw = 512
    for n in range(D_MODEL // cw):
        lo = cw * n

        def heads(o_ref, w_ref):
            acc = jnp.dot(o_ref[0], w_ref[0:LANES, lo:lo + cw], preferred_element_type=f32)
            for hh in range(1, N_HEADS):
                acc += jnp.dot(o_ref[hh], w_ref[LANES * hh:LANES * (hh + 1), lo:lo + cw],
                               preferred_element_type=f32)
            return acc

        def gate(which):
            z = jnp.dot(h, wg_ref[:, D_MODEL * which + lo:D_MODEL * which + lo + cw],
                        preferred_element_type=f32)
            return jax.nn.sigmoid(z)

        mg = gate(0) * heads(oa_ref, wpa_ref)
        mg += gate(1) * heads(ob_ref, wpb_ref)
        mg += gate(2) * jnp.dot(oc, wpc_ref[:, lo:lo + cw], preferred_element_type=f32)
        mg_ref[:, lo:lo + cw] = mg.astype(bf16)
    x = x + jnp.dot(mg_ref[...], wo_ref[...], preferred_element_type=f32)

    h2 = _rms(jnp.concatenate([halo_ref[...], x], axis=0), g2_ref[...]).astype(bf16)
    halo_ref[...] = x[TM - HALO:TM]

    def conv(lo):
        u_ref[...] = jnp.dot(h2, wup_ref[:, lo:lo + FF_CHUNK], preferred_element_type=f32)
        w = cw_ref[:, lo:lo + FF_CHUNK]
        return (w[2:3] * u_ref[HALO:HALO + TM] + w[1:2] * u_ref[HALO - 1:HALO - 1 + TM]
                + w[0:1] * u_ref[HALO - 2:HALO - 2 + TM] + cb_ref[:, lo:lo + FF_CHUNK])

    for c in range(D_FF // FF_CHUNK):
        a = conv(FF_CHUNK * c)
        b = conv(D_FF + FF_CHUNK * c)
        act_ref[:, FF_CHUNK * c:FF_CHUNK * (c + 1)] = (a * jax.nn.sigmoid(a) * b).astype(bf16)
    o = x + jnp.dot(act_ref[...], wdn_ref[...], preferred_element_type=f32)
    if final:
        o = _rms(o, gf_ref[...])
    out_ref[...] = o


def _mix_ffn(x2, g1, wg, oa, ob, oc, wpa, wpb, wpc, wo, g2, wup, cw, cb, wdn, gf, final):
    s = x2.shape[0]
    hb = pl.BlockSpec((N_HEADS, TM, LANES), lambda i: (0, i, 0))
    full = lambda a: _resident(a.shape, lambda i: (0,) * a.ndim)
    vec = lambda a: pl.BlockSpec(a.shape, lambda i: (0,) * a.ndim)
    row = lambda w: pl.BlockSpec((TM, w), lambda i: (i, 0))
    return pl.pallas_call(
        functools.partial(_mix_ffn_kernel, final=final),
        grid=(s // TM,),
        in_specs=[row(D_MODEL), vec(g1), full(wg), hb, hb, row(512),
                  full(wpa), full(wpb), full(wpc), full(wo),
                  vec(g2), full(wup), vec(cw), vec(cb), full(wdn), vec(gf)],
        out_specs=row(D_MODEL),
        out_shape=jax.ShapeDtypeStruct((s, D_MODEL), f32),
        scratch_shapes=[pltpu.VMEM((TM, D_MODEL), bf16), pltpu.VMEM((HALO, D_MODEL), f32),
                        pltpu.VMEM((TM + HALO, FF_CHUNK), f32), pltpu.VMEM((TM, D_FF), bf16)],
        compiler_params=_params("arbitrary"),
        name="mix_ffn",
    )(x2, g1, wg, oa, ob, oc, wpa, wpb, wpc, wo, g2, wup, cw, cb, wdn, gf)


def _pad_head_rows(w, nh):
    l, _, d = w.shape
    w = jnp.pad(w.reshape(l, nh, HEAD_DIM, d), ((0, 0), (0, 0), (0, LANES - HEAD_DIM), (0, 0)))
    return w.reshape(l, nh * LANES, d)


def _fused_in_weight(w_in):
    l = w_in.shape[0]
    cuts = np.cumsum([0, 256, 256, 256, 256, 256, 256, N_HEADS, 512, 128, 128]).tolist()
    fb, qc, kc, vc = (w_in[..., cuts[n]:cuts[n + 1]] for n in range(6, 10))
    qc = (qc * SCALE).reshape(l, D_MODEL, C_HEADS, HEAD_DIM)
    zero = jnp.zeros_like(qc[:, :, 0])
    qc_pad = jnp.concatenate(
        [jnp.concatenate([qc[:, :, hq], zero] if hq < C_GROUP else [zero, qc[:, :, hq]], axis=-1)
         for hq in range(C_HEADS)], axis=-1)
    vc2 = jnp.concatenate([vc, vc[..., HEAD_DIM:], vc[..., :HEAD_DIM]], axis=-1)
    fbp = jnp.pad(fb, ((0, 0), (0, 0), (0, LANES - N_HEADS)))
    w1 = jnp.concatenate([w_in[..., :cuts[6]], qc_pad, kc, vc2, fbp], axis=-1)
    assert w1.shape[-1] == _W1_COLS
    return w1.astype(bf16), w_in[..., cuts[10]:].astype(bf16)


def _bias_lookup(tbl, dist):
    bucket = _t5_bucket(jnp.asarray(dist, jnp.int32))
    out = jnp.zeros((tbl.shape[0],) + bucket.shape, f32)
    for b in range(NUM_BUCKETS):
        col = tbl[:, b].astype(f32).reshape((-1,) + (1,) * bucket.ndim)
        out = jnp.where(bucket[None] == b, col, out)
    return out


def _toeplitz_kernel(u_ref, o_ref):
    x = jnp.broadcast_to(u_ref[0, 0], (TA, 2 * TA))
    o_ref[0, 0] = pltpu.roll(x, 0, 1, stride=1, stride_axis=0)[:, :TA]


def _moba_bias_tiles(tbl_a, s):
    near = min(-(-(MAX_DISTANCE + TA - 1) // TA), s // TA)
    m = np.arange(2 * TA)
    rel = np.where(m < TA, -m, np.where(m == TA, 0, 2 * TA - m))
    dist = np.arange(near)[:, None] * TA + rel[None, :]
    gen = jnp.where(dist >= 0, _bias_lookup(tbl_a, np.maximum(dist, 0)), NEG)
    tiles = pl.pallas_call(
        _toeplitz_kernel,
        grid=(N_HEADS, near),
        in_specs=[pl.BlockSpec((1, 1, 1, 2 * TA), lambda h, d: (h, d, 0, 0))],
        out_specs=pl.BlockSpec((1, 1, TA, TA), lambda h, d: (h, d, 0, 0)),
        out_shape=jax.ShapeDtypeStruct((N_HEADS, near, TA, TA), f32),
        compiler_params=_params("arbitrary", "arbitrary"),
        name="moba_bias_tiles",
    )(gen[:, :, None, :])
    far = _bias_lookup(tbl_a, np.asarray([near * TA]))[:, 0]
    return tiles, far


def _swa_tables(tbl_c, sinks):
    tl = np.arange(Q_BLOCK)
    sl = np.arange(2 * Q_BLOCK)
    dist = tl[:, None] + Q_BLOCK - sl[None, :]
    bias = _bias_lookup(tbl_c, dist)
    band = (dist >= 0) & (dist < WINDOW)
    masks = np.stack([band, band & (sl[None, :] >= Q_BLOCK)])
    table = jnp.where(masks[:, None], bias[None], NEG)
    table = table.reshape(2, C_HEADS // C_GROUP, C_GROUP * Q_BLOCK, 2 * Q_BLOCK).astype(f32)
    sink = jnp.repeat(sinks.astype(f32), Q_BLOCK, axis=-1)
    sink = sink.reshape(sinks.shape[0], C_HEADS // C_GROUP, C_GROUP * Q_BLOCK, 1)
    return table, sink


def kernel(x, ln1, w_in, b_f, sinks, w_pa, w_pb, w_pc, w_o, ln2, w_up, conv_w, conv_b, w_down,
           rel_bias, ln_f):
    bsz, s, _ = x.shape
    assert bsz == 1 and s % TM == 0 and s % TA == 0 and TM % MOBA_BLOCK == 0
    assert s // MOBA_BLOCK <= HEAD_DIM
    w1, wg = _fused_in_weight(w_in)
    wpa = _pad_head_rows(w_pa, N_HEADS).astype(bf16)
    wpb = _pad_head_rows(w_pb, N_HEADS).astype(bf16)
    wpc, wo, wup, wdn = (w.astype(bf16) for w in (w_pc, w_o, w_up, w_down))
    tbl_a = rel_bias[:, :N_HEADS].T
    tbl_c = rel_bias[:, N_HEADS:].T
    moba_tiles, moba_far = _moba_bias_tiles(tbl_a, s)
    swa_bias, swa_sink = _swa_tables(tbl_c, sinks)
    bf_lane = jnp.pad(b_f.astype(f32), ((0, 0), (0, LANES - N_HEADS)))[:, None, :]
    gf = ln_f.astype(f32)[None, :]

    x2 = x[0]
    for l in range(DEPTH):
        qa, ka, va, qb, kb, vb, c = _inproj(x2, ln1[l][None, :], w1[l], bf_lane[l])
        oa = _moba_flash(moba_far, qa, ka, va, moba_tiles)
        ob = _fox_flash(qb, kb, vb)
        oc = _swa(c, swa_bias, swa_sink[l])
        x2 = _mix_ffn(x2, ln1[l][None, :], wg[l], oa, ob, oc, wpa[l], wpb[l], wpc[l], wo[l],
                      ln2[l][None, :], wup[l], conv_w[l], conv_b[l][None, :], wdn[l], gf,
                      final=(l == DEPTH - 1))
    return x2[None]
```

```python
import functools
import math

import numpy as np
import jax
import jax.numpy as jnp
from jax import lax
from jax.experimental import pallas as pl
from jax.experimental.pallas import tpu as pltpu

D_MODEL = 1024
DEPTH = 4
HEAD_DIM = 64
N_HEADS = 4
C_HEADS = 8
C_GROUP = 4
MOBA_BLOCK = 256
MOBA_TOPK = 3
Q_BLOCK = 128
WINDOW = 128
NUM_BUCKETS = 32
MAX_DISTANCE = 4096
D_FF = 2816
EPS = 1e-6
NEG = -1e30
SCALE = HEAD_DIM ** -0.5

LANES = 128
HALO = 8
TM = 512
TA = 1024
FF_CHUNK = 256
VMEM_LIMIT = 56 * 1024 * 1024

_QKV_A = (0, 768)
_QKV_B = (768, 1536)
_C_ALL = (1536, 2944)
_F_B = (2944, 3072)
_W1_COLS = 3072
_C_K = 1024
_C_V = 1152
_C_W = 1408

bf16 = jnp.bfloat16
f32 = jnp.float32
_HI = lax.Precision.HIGHEST


def _rms(x, g):
    ms = jnp.mean(x * x, axis=-1, keepdims=True)
    return x * lax.rsqrt(ms + EPS) * g


def _t5_bucket(dist):
    dist = jnp.maximum(dist, 0)
    max_exact = NUM_BUCKETS // 2
    d = jnp.maximum(dist.astype(jnp.float32), 1.0)
    large = max_exact + (jnp.log(d / max_exact) / math.log(MAX_DISTANCE / max_exact)
                         * (NUM_BUCKETS - max_exact)).astype(jnp.int32)
    large = jnp.minimum(large, NUM_BUCKETS - 1)
    return jnp.where(dist < max_exact, dist, large)


def _params(*sem):
    return pltpu.CompilerParams(dimension_semantics=sem, vmem_limit_bytes=VMEM_LIMIT)


def _resident(shape, index_map):
    return pl.BlockSpec(shape, index_map, pipeline_mode=pl.Buffered(1))


def _layer_weight(w, layer):
    return _resident((None,) + w.shape[1:], lambda *_: (layer,) + (0,) * (w.ndim - 1))


def _head_lanes(z, h, part):
    lo = N_HEADS * HEAD_DIM * part + LANES * (h // 2)
    col = z[:, lo:lo + LANES]
    return pltpu.roll(col, HEAD_DIM, 1) if h % 2 else col


def _moba_operands(i, za, km_ref, q_ref, k_ref, v_ref, nblk):
    width = N_HEADS * HEAD_DIM
    lane = lax.broadcasted_iota(jnp.int32, (TM, LANES), 1)
    kblk = (i * TM + lax.broadcasted_iota(jnp.int32, (TM, LANES), 0)) >> 8
    bidx = lax.broadcasted_iota(jnp.int32, (nblk, TM), 0)
    bidx_f = bidx.astype(f32)
    qblk = (i * TM + lax.broadcasted_iota(jnp.int32, (nblk, TM), 1)) >> 8
    member = jnp.where(bidx == qblk, 1.0 / MOBA_BLOCK, 0.0).astype(f32)
    km_ref[...] += jnp.dot(member, za[:, width:2 * width], precision=_HI, preferred_element_type=f32)
    km = km_ref[...]
    km_head = lax.broadcasted_iota(jnp.int32, (nblk, width), 1) >> 6
    q_all = za[:, 0:width]
    for h in range(N_HEADS):
        gate = lax.dot_general(jnp.where(km_head == h, km, 0.0), q_all, (((1,), (1,)), ((), ())),
                               precision=_HI, preferred_element_type=f32)
        gate = jnp.where(bidx < qblk, gate, -jnp.inf)
        sel = jnp.zeros((nblk, TM), f32)
        for _ in range(MOBA_TOPK):
            m = jnp.max(gate, axis=0, keepdims=True)
            first = jnp.min(jnp.where(gate == m, bidx_f, float(nblk)), axis=0, keepdims=True)
            hit = jnp.logical_and(bidx_f == first, m > -jnp.inf)
            sel = jnp.where(hit, 1.0, sel)
            gate = jnp.where(hit, -jnp.inf, gate)
        notsel = jnp.where(bidx == qblk, 0.0, 1.0 - sel)
        pad = jnp.zeros((LANES - nblk, TM), f32)
        if nblk == HEAD_DIM:
            ns = jnp.concatenate([pad, notsel], axis=0).T
        else:
            ns = jnp.concatenate([pad[:HEAD_DIM], notsel, pad[HEAD_DIM:]], axis=0).T
        q_ref[h] = jnp.where(lane < HEAD_DIM, _head_lanes(za, h, 0) * SCALE, ns).astype(bf16)
        k_ref[h] = jnp.where(lane < HEAD_DIM, _head_lanes(za, h, 1),
                             jnp.where(lane == HEAD_DIM + kblk, NEG, 0.0)).astype(bf16)
        v_ref[h] = jnp.where(lane < HEAD_DIM, _head_lanes(za, h, 2),
                             jnp.where(lane == HEAD_DIM, 1.0, 0.0)).astype(bf16)


def _fox_operands(zb, fb, cum_ref, q_ref, k_ref, v_ref):
    logf = jnp.minimum(fb, 0.0) - jnp.log1p(jnp.exp(-jnp.abs(fb)))
    r = lax.broadcasted_iota(jnp.int32, (TM, TM), 0)
    c = lax.broadcasted_iota(jnp.int32, (TM, TM), 1)
    tri = jnp.where(c <= r, 1.0, 0.0).astype(f32)
    cum = jnp.dot(tri, logf, precision=_HI, preferred_element_type=f32) + cum_ref[...]
    cum_ref[...] = cum[TM - 1:TM, :]
    lane = lax.broadcasted_iota(jnp.int32, (TM, LANES), 1)
    d = HEAD_DIM
    for h in range(N_HEADS):
        cb = jnp.broadcast_to(cum[:, h:h + 1], (TM, LANES))
        hi = cb.astype(bf16).astype(f32)
        r1 = cb - hi
        mid = r1.astype(bf16).astype(f32)
        lo = r1 - mid
        qa = jnp.where(lane < d, _head_lanes(zb, h, 0) * SCALE,
             jnp.where(lane == d, hi,
             jnp.where(lane == d + 1, mid,
             jnp.where(lane == d + 2, lo,
             jnp.where(lane < d + 6, 1.0, 0.0)))))
        q_ref[h] = qa.astype(bf16)
        ka = jnp.where(lane < d, _head_lanes(zb, h, 1),
             jnp.where(lane < d + 3, 1.0,
             jnp.where(lane == d + 3, -hi,
             jnp.where(lane == d + 4, -mid,
             jnp.where(lane == d + 5, -lo, 0.0)))))
        k_ref[h] = ka.astype(bf16)
        v_ref[h] = jnp.where(lane < d, _head_lanes(zb, h, 2),
                             jnp.where(lane == d, 1.0, 0.0)).astype(bf16)


def _inproj_kernel(x_ref, g_ref, w_ref, bf_ref, qa_ref, ka_ref, va_ref, qb_ref, kb_ref, vb_ref, c_ref,
                   km_ref, cum_ref, *, nblk):
    i = pl.program_id(0)

    @pl.when(i == 0)
    def _():
        km_ref[...] = jnp.zeros_like(km_ref)
        cum_ref[...] = jnp.zeros_like(cum_ref)

    h = _rms(x_ref[...], g_ref[...]).astype(bf16)

    def mm(span):
        return jnp.dot(h, w_ref[:, span[0]:span[1]], preferred_element_type=f32)

    _moba_operands(i, mm(_QKV_A), km_ref, qa_ref, ka_ref, va_ref, nblk)
    _fox_operands(mm(_QKV_B), mm(_F_B) + bf_ref[...], cum_ref, qb_ref, kb_ref, vb_ref)
    c_ref[...] = mm(_C_ALL).astype(bf16)


def _inproj(x2, g, w1, bf_lane, layer):
    s = x2.shape[0]
    nblk = s // MOBA_BLOCK
    hb = pl.BlockSpec((N_HEADS, TM, LANES), lambda i: (0, i, 0))
    aug = jax.ShapeDtypeStruct((N_HEADS, s, LANES), bf16)
    return pl.pallas_call(
        functools.partial(_inproj_kernel, nblk=nblk),
        grid=(s // TM,),
        in_specs=[pl.BlockSpec((TM, D_MODEL), lambda i: (i, 0)),
                  pl.BlockSpec((1, D_MODEL), lambda i: (0, 0)),
                  _layer_weight(w1, layer),
                  pl.BlockSpec((1, LANES), lambda i: (0, 0))],
        out_specs=[hb] * 6 + [pl.BlockSpec((TM, _C_W), lambda i: (i, 0))],
        out_shape=[aug] * 6 + [jax.ShapeDtypeStruct((s, _C_W), bf16)],
        scratch_shapes=[pltpu.VMEM((nblk, N_HEADS * HEAD_DIM), f32), pltpu.VMEM((1, LANES), f32)],
        compiler_params=_params("arbitrary"),
        name="inproj",
    )(x2, g, w1, bf_lane)


def _flash_step(q, k_ref, v_ref, j, carry, add_bias):
    m, acc = carry
    off = pl.multiple_of(j * TA, TA)
    k = k_ref[0, pl.ds(off, TA), :]
    v = v_ref[0, pl.ds(off, TA), :]
    s = lax.dot_general(q, k, (((1,), (1,)), ((), ())), preferred_element_type=f32)
    s = add_bias(s)
    m_new = jnp.maximum(m, jnp.max(s, axis=-1, keepdims=True))
    alpha = jnp.exp(m - m_new)
    p = jnp.exp(s - m_new)
    acc = alpha * acc + jnp.dot(p.astype(bf16), v, preferred_element_type=f32)
    return m_new, acc


def _flash_finish(o_ref, acc):
    o_ref[0] = (acc / acc[:, HEAD_DIM:HEAD_DIM + 1]).astype(o_ref.dtype)


def _flash_init():
    return jnp.full((TA, 1), -jnp.inf, f32), jnp.zeros((TA, LANES), f32)


def _fox_flash_kernel(q_ref, k_ref, v_ref, o_ref):
    i = pl.program_id(1)
    q = q_ref[0]
    carry = lax.fori_loop(
        0, i, lambda j, c: _flash_step(q, k_ref, v_ref, j, c, lambda s: s), _flash_init())
    row = lax.broadcasted_iota(jnp.int32, (TA, TA), 0)
    col = lax.broadcasted_iota(jnp.int32, (TA, TA), 1)
    _, acc = _flash_step(q, k_ref, v_ref, i, carry, lambda s: jnp.where(col <= row, s, NEG))
    _flash_finish(o_ref, acc)


def _moba_flash_kernel(far_ref, q_ref, k_ref, v_ref, bias_ref, o_ref, *, near):
    h = pl.program_id(0)
    i = pl.program_id(1)
    q = q_ref[0]
    far = far_ref[h]
    n_far = jnp.maximum(i - (near - 1), 0)
    carry = lax.fori_loop(
        0, n_far, lambda j, c: _flash_step(q, k_ref, v_ref, j, c, lambda s: s + far), _flash_init())
    carry = lax.fori_loop(
        n_far, i + 1,
        lambda j, c: _flash_step(q, k_ref, v_ref, j, c, lambda s: s + bias_ref[0, i - j]), carry)
    _flash_finish(o_ref, carry[1])


def _flash_specs(s):
    qspec = pl.BlockSpec((1, TA, LANES), lambda h, i: (h, i, 0))
    kvspec = _resident((1, s, LANES), lambda h, i: (h, 0, 0))
    return qspec, kvspec


def _fox_flash(q, k, v):
    s = q.shape[1]
    qspec, kvspec = _flash_specs(s)
    return pl.pallas_call(
        _fox_flash_kernel,
        grid=(N_HEADS, s // TA),
        in_specs=[qspec, kvspec, kvspec],
        out_specs=qspec,
        out_shape=jax.ShapeDtypeStruct((N_HEADS, s, LANES), bf16),
        compiler_params=_params("arbitrary", "arbitrary"),
        name="fox_flash",
    )(q, k, v)


def _moba_flash(far, q, k, v, bias):
    s = q.shape[1]
    near = bias.shape[1]
    qspec, kvspec = _flash_specs(s)
    return pl.pallas_call(
        functools.partial(_moba_flash_kernel, near=near),
        grid=(N_HEADS, s // TA),
        in_specs=[pl.BlockSpec(memory_space=pltpu.SMEM), qspec, kvspec, kvspec,
                  _resident((1, near, TA, TA), lambda h, i: (h, 0, 0, 0))],
        out_specs=qspec,
        out_shape=jax.ShapeDtypeStruct((N_HEADS, s, LANES), bf16),
        compiler_params=_params("arbitrary", "arbitrary"),
        name="moba_flash",
    )(far, q, k, v, bias)


def _swa_kernel(cur_ref, prev_ref, bias_ref, o_ref):
    i = pl.program_id(0)
    first = jnp.where(i == 0, 1, 0)
    lane = lax.broadcasted_iota(jnp.int32, (Q_BLOCK, LANES), 1)
    qb = Q_BLOCK
    ones = jnp.ones((2 * qb, LANES), bf16)
    for b in range(TM // qb):
        if b == 0:
            kband = jnp.concatenate([prev_ref[TM - qb:TM, _C_K:_C_V], cur_ref[0:qb, _C_K:_C_V]], axis=0)
            vband = jnp.concatenate([prev_ref[TM - qb:TM, _C_V:_C_W], cur_ref[0:qb, _C_V:_C_W]], axis=0)
            variant = first
        else:
            kband = cur_ref[qb * (b - 1):qb * (b + 1), _C_K:_C_V]
            vband = cur_ref[qb * (b - 1):qb * (b + 1), _C_V:_C_W]
            variant = 0
        kband = jnp.where(lax.broadcasted_iota(jnp.int32, kband.shape, 0) == 0, 0.0, kband).astype(bf16)
        vband = jnp.where(lax.broadcasted_iota(jnp.int32, vband.shape, 0) == 0, 0.0, vband).astype(bf16)
        outs = []
        for g in range(C_HEADS // C_GROUP):
            qs = jnp.concatenate(
                [cur_ref[qb * b:qb * (b + 1), LANES * (C_GROUP * g + u):LANES * (C_GROUP * g + u + 1)]
                 for u in range(C_GROUP)], axis=0)
            s = lax.dot_general(qs, kband, (((1,), (1,)), ((), ())), preferred_element_type=f32)
            s = s + bias_ref[variant, g]
            p = jnp.exp(s - jnp.max(s, axis=-1, keepdims=True)).astype(bf16)
            den = jnp.dot(p, ones, preferred_element_type=f32)
            for u in range(C_GROUP):
                half = (C_GROUP * g + u) % 2
                vg = vband[:, 0:LANES] if g == half else vband[:, LANES:2 * LANES]
                pv = jnp.dot(p[qb * u:qb * (u + 1)], vg, preferred_element_type=f32)
                outs.append(pv / den[qb * u:qb * (u + 1)])
        for pr in range(C_HEADS // 2):
            o_ref[qb * b:qb * (b + 1), LANES * pr:LANES * (pr + 1)] = jnp.where(
                lane < HEAD_DIM, outs[2 * pr], outs[2 * pr + 1]).astype(o_ref.dtype)


def _swa(c, bias, layer):
    s = c.shape[0]
    return pl.pallas_call(
        _swa_kernel,
        grid=(s // TM,),
        in_specs=[pl.BlockSpec((TM, _C_W), lambda i: (i, 0)),
                  pl.BlockSpec((TM, _C_W), lambda i: (jnp.maximum(i - 1, 0), 0)),
                  pl.BlockSpec((None,) + bias.shape[1:], lambda i: (layer, 0, 0, 0, 0))],
        out_specs=pl.BlockSpec((TM, 512), lambda i: (i, 0)),
        out_shape=jax.ShapeDtypeStruct((s, 512), bf16),
        compiler_params=_params("arbitrary"),
        name="swa",
    )(c, c, bias)


def _mix_ffn_kernel(x_ref, g1_ref, wg_ref, oa_ref, ob_ref, oc_ref, wpa_ref, wpb_ref, wpc_ref, wo_ref,
                    g2_ref, wup_ref, cw_ref, cb_ref, wdn_ref, gf_ref, out_ref,
                    mg_ref, halo_ref, u_ref, act_ref, *, final):
    @pl.when(pl.program_id(0) == 0)
    def _():
        halo_ref[...] = jnp.zeros_like(halo_ref)

    x = x_ref[...]
    h = _rms(x, g1_ref[...]).astype(bf16)
    oc = oc_ref[...]
    cw = 512
    for n in range(D_MODEL // cw):
        lo = cw * n

        def heads(o_ref, w_ref):
            acc = jnp.dot(o_ref[0], w_ref[0:LANES, lo:lo + cw], preferred_element_type=f32)
            for hh in range(1, N_HEADS):
                acc += jnp.dot(o_ref[hh], w_ref[LANES * hh:LANES * (hh + 1), lo:lo + cw],
                               preferred_element_type=f32)
            return acc

        def gate(which):
            z = jnp.dot(h, wg_ref[:, D_MODEL * which + lo:D_MODEL * which + lo + cw],
                        preferred_element_type=f32)
            return jax.nn.sigmoid(z)

        mg = gate(0) * heads(oa_ref, wpa_ref)
        mg += gate(1) * heads(ob_ref, wpb_ref)
        mg += gate(2) * jnp.dot(oc, wpc_ref[:, lo:lo + cw], preferred_element_type=f32)
        mg_ref[:, lo:lo + cw] = mg.astype(bf16)
    x = x + jnp.dot(mg_ref[...], wo_ref[...], preferred_element_type=f32)

    h2 = _rms(jnp.concatenate([halo_ref[...], x], axis=0), g2_ref[...]).astype(bf16)
    halo_ref[...] = x[TM - HALO:TM]

    def conv(lo):
        u_ref[...] = jnp.dot(h2, wup_ref[:, lo:lo + FF_CHUNK], preferred_element_type=f32)
        w = cw_ref[:, lo:lo + FF_CHUNK]
        return (w[2:3] * u_ref[HALO:HALO + TM] + w[1:2] * u_ref[HALO - 1:HALO - 1 + TM]
                + w[0:1] * u_ref[HALO - 2:HALO - 2 + TM] + cb_ref[:, lo:lo + FF_CHUNK])

    for c in range(D_FF // FF_CHUNK):
        a = conv(FF_CHUNK * c)
        b = conv(D_FF + FF_CHUNK * c)
        act_ref[:, FF_CHUNK * c:FF_CHUNK * (c + 1)] = (a * jax.nn.sigmoid(a) * b).astype(bf16)
    o = x + jnp.dot(act_ref[...], wdn_ref[...], preferred_element_type=f32)
    if final:
        o = _rms(o, gf_ref[...])
    out_ref[...] = o


def _mix_ffn(x2, g1, wg, oa, ob, oc, wpa, wpb, wpc, wo, g2, wup, cw, cb, wdn, gf, layer):
    s = x2.shape[0]
    hb = pl.BlockSpec((N_HEADS, TM, LANES), lambda i: (0, i, 0))
    full = lambda a: _layer_weight(a, layer)
    vec = lambda a: pl.BlockSpec(a.shape, lambda i: (0,) * a.ndim)
    row = lambda w: pl.BlockSpec((TM, w), lambda i: (i, 0))
    return pl.pallas_call(
        functools.partial(_mix_ffn_kernel, final=(layer == DEPTH - 1)),
        grid=(s // TM,),
        in_specs=[row(D_MODEL), vec(g1), full(wg), hb, hb, row(512),
                  full(wpa), full(wpb), full(wpc), full(wo),
                  vec(g2), full(wup), vec(cw), vec(cb), full(wdn), vec(gf)],
        out_specs=row(D_MODEL),
        out_shape=jax.ShapeDtypeStruct((s, D_MODEL), f32),
        scratch_shapes=[pltpu.VMEM((TM, D_MODEL), bf16), pltpu.VMEM((HALO, D_MODEL), f32),
                        pltpu.VMEM((TM + HALO, FF_CHUNK), f32), pltpu.VMEM((TM, D_FF), bf16)],
        compiler_params=_params("arbitrary"),
        name="mix_ffn",
    )(x2, g1, wg, oa, ob, oc, wpa, wpb, wpc, wo, g2, wup, cw, cb, wdn, gf)


def _pad_head_rows(w, nh):
    l, _, d = w.shape
    w = jnp.pad(w.reshape(l, nh, HEAD_DIM, d), ((0, 0), (0, 0), (0, LANES - HEAD_DIM), (0, 0)))
    return w.reshape(l, nh * LANES, d)


def _fused_in_weight(w_in):
    l = w_in.shape[0]
    cuts = np.cumsum([0, 256, 256, 256, 256, 256, 256, N_HEADS, 512, 128, 128]).tolist()
    fb, qc, kc, vc = (w_in[..., cuts[n]:cuts[n + 1]] for n in range(6, 10))
    qc = (qc * SCALE).reshape(l, D_MODEL, C_HEADS, HEAD_DIM)
    zero = jnp.zeros_like(qc[:, :, 0])
    qc_pad = jnp.concatenate(
        [jnp.concatenate([qc[:, :, hq], zero] if hq < C_GROUP else [zero, qc[:, :, hq]], axis=-1)
         for hq in range(C_HEADS)], axis=-1)
    vc2 = jnp.concatenate([vc, vc[..., HEAD_DIM:], vc[..., :HEAD_DIM]], axis=-1)
    fbp = jnp.pad(fb, ((0, 0), (0, 0), (0, LANES - N_HEADS)))
    w1 = jnp.concatenate([w_in[..., :cuts[6]], qc_pad, kc, vc2, fbp], axis=-1)
    assert w1.shape[-1] == _W1_COLS
    return w1.astype(bf16), w_in[..., cuts[10]:].astype(bf16)


def _bias_lookup(tbl, dist):
    bucket = _t5_bucket(jnp.asarray(dist, jnp.int32))
    out = jnp.zeros((tbl.shape[0],) + bucket.shape, f32)
    for b in range(NUM_BUCKETS):
        col = tbl[:, b].astype(f32).reshape((-1,) + (1,) * bucket.ndim)
        out = jnp.where(bucket[None] == b, col, out)
    return out


def _toeplitz_kernel(u_ref, o_ref):
    x = jnp.broadcast_to(u_ref[0, 0], (TA, 2 * TA))
    o_ref[0, 0] = pltpu.roll(x, 0, 1, stride=1, stride_axis=0)[:, :TA]


def _moba_bias_tiles(tbl_a, s):
    near = min(-(-(MAX_DISTANCE + TA - 1) // TA), s // TA)
    m = np.arange(2 * TA)
    rel = np.where(m < TA, -m, np.where(m == TA, 0, 2 * TA - m))
    dist = np.arange(near)[:, None] * TA + rel[None, :]
    gen = jnp.where(dist >= 0, _bias_lookup(tbl_a, np.maximum(dist, 0)), NEG)
    tiles = pl.pallas_call(
        _toeplitz_kernel,
        grid=(N_HEADS, near),
        in_specs=[pl.BlockSpec((1, 1, 1, 2 * TA), lambda h, d: (h, d, 0, 0))],
        out_specs=pl.BlockSpec((1, 1, TA, TA), lambda h, d: (h, d, 0, 0)),
        out_shape=jax.ShapeDtypeStruct((N_HEADS, near, TA, TA), f32),
        compiler_params=_params("arbitrary", "arbitrary"),
        name="moba_bias_tiles",
    )(gen[:, :, None, :])
    far = _bias_lookup(tbl_a, np.asarray([near * TA]))[:, 0]
    return tiles, far


def _swa_tables(tbl_c, sinks):
    tl = np.arange(Q_BLOCK)
    sl = np.arange(2 * Q_BLOCK)
    dist = tl[:, None] + Q_BLOCK - sl[None, :]
    bias = _bias_lookup(tbl_c, dist)
    band = (dist >= 0) & (dist < WINDOW)
    assert not band[:, 0].any()
    masks = np.stack([band, band & (sl[None, :] >= Q_BLOCK)])
    table = jnp.where(masks[:, None], bias[None], NEG)
    sink = sinks.astype(f32)[:, None, :, None, None]
    table = jnp.where((sl == 0)[None, None, None, None, :], sink, table[None])
    return table.reshape(sinks.shape[0], 2, C_HEADS // C_GROUP, C_GROUP * Q_BLOCK, 2 * Q_BLOCK)


def kernel(x, ln1, w_in, b_f, sinks, w_pa, w_pb, w_pc, w_o, ln2, w_up, conv_w, conv_b, w_down,
           rel_bias, ln_f):
    bsz, s, _ = x.shape
    assert bsz == 1 and s % TM == 0 and s % TA == 0 and TM % MOBA_BLOCK == 0
    assert s // MOBA_BLOCK <= HEAD_DIM
    w1, wg = _fused_in_weight(w_in)
    wpa = _pad_head_rows(w_pa, N_HEADS).astype(bf16)
    wpb = _pad_head_rows(w_pb, N_HEADS).astype(bf16)
    wpc, wo, wup, wdn = (w.astype(bf16) for w in (w_pc, w_o, w_up, w_down))
    tbl_a = rel_bias[:, :N_HEADS].T
    tbl_c = rel_bias[:, N_HEADS:].T
    moba_tiles, moba_far = _moba_bias_tiles(tbl_a, s)
    swa_bias = _swa_tables(tbl_c, sinks)
    bf_lane = jnp.pad(b_f.astype(f32), ((0, 0), (0, LANES - N_HEADS)))[:, None, :]
    gf = ln_f.astype(f32)[None, :]

    x2 = x[0]
    for l in range(DEPTH):
        qa, ka, va, qb, kb, vb, c = _inproj(x2, ln1[l][None, :], w1, bf_lane[l], l)
        oa = _moba_flash(moba_far, qa, ka, va, moba_tiles)
        ob = _fox_flash(qb, kb, vb)
        oc = _swa(c, swa_bias, l)
        x2 = _mix_ffn(x2, ln1[l][None, :], wg, oa, ob, oc, wpa, wpb, wpc, wo,
                      ln2[l][None, :], wup, conv_w[l], conv_b[l][None, :], wdn, gf, l)
    return x2[None]
```

```python
import functools
import math

import numpy as np
import jax
import jax.numpy as jnp
from jax import lax
from jax.experimental import pallas as pl
from jax.experimental.pallas import tpu as pltpu

D_MODEL = 1024
DEPTH = 4
HEAD_DIM = 64
N_HEADS = 4
C_HEADS = 8
C_GROUP = 4
MOBA_BLOCK = 256
MOBA_TOPK = 3
Q_BLOCK = 128
WINDOW = 128
NUM_BUCKETS = 32
MAX_DISTANCE = 4096
D_FF = 2816
EPS = 1e-6
NEG = -1e30
SCALE = HEAD_DIM ** -0.5

LANES = 128
HALO = 8
TM = 512
TA = 1024
FLASH_UNROLL = 4
FF_CHUNK = 256
VMEM_LIMIT = 56 * 1024 * 1024

_QKV_A = (0, 768)
_QKV_B = (768, 1536)
_C_ALL = (1536, 2944)
_F_B = (2944, 3072)
_W1_COLS = 3072
_C_K = 1024
_C_V = 1152
_C_W = 1408

bf16 = jnp.bfloat16
f32 = jnp.float32
_HI = lax.Precision.HIGHEST


def _rms(x, g):
    ms = jnp.mean(x * x, axis=-1, keepdims=True)
    return x * lax.rsqrt(ms + EPS) * g


def _t5_bucket(dist):
    dist = jnp.maximum(dist, 0)
    max_exact = NUM_BUCKETS // 2
    d = jnp.maximum(dist.astype(jnp.float32), 1.0)
    large = max_exact + (jnp.log(d / max_exact) / math.log(MAX_DISTANCE / max_exact)
                         * (NUM_BUCKETS - max_exact)).astype(jnp.int32)
    large = jnp.minimum(large, NUM_BUCKETS - 1)
    return jnp.where(dist < max_exact, dist, large)


def _params(*sem):
    return pltpu.CompilerParams(dimension_semantics=sem, vmem_limit_bytes=VMEM_LIMIT)


def _resident(shape, index_map):
    return pl.BlockSpec(shape, index_map, pipeline_mode=pl.Buffered(1))


def _layer_weight(w, layer):
    return _resident((None,) + w.shape[1:], lambda *_: (layer,) + (0,) * (w.ndim - 1))


def _head_lanes(z, h, part):
    lo = N_HEADS * HEAD_DIM * part + LANES * (h // 2)
    col = z[:, lo:lo + LANES]
    return pltpu.roll(col, HEAD_DIM, 1) if h % 2 else col


def _moba_operands(i, za, km_ref, q_ref, k_ref, v_ref, nblk):
    width = N_HEADS * HEAD_DIM
    lane = lax.broadcasted_iota(jnp.int32, (TM, LANES), 1)
    kblk = (i * TM + lax.broadcasted_iota(jnp.int32, (TM, LANES), 0)) >> 8
    bidx = lax.broadcasted_iota(jnp.int32, (nblk, TM), 0)
    bidx_f = bidx.astype(f32)
    qblk = (i * TM + lax.broadcasted_iota(jnp.int32, (nblk, TM), 1)) >> 8
    member = jnp.where(bidx == qblk, 1.0 / MOBA_BLOCK, 0.0).astype(f32)
    km_ref[...] += jnp.dot(member, za[:, width:2 * width], precision=_HI, preferred_element_type=f32)
    km = km_ref[...]
    km_head = lax.broadcasted_iota(jnp.int32, (nblk, width), 1) >> 6
    q_all = za[:, 0:width]
    for h in range(N_HEADS):
        gate = lax.dot_general(jnp.where(km_head == h, km, 0.0), q_all, (((1,), (1,)), ((), ())),
                               precision=_HI, preferred_element_type=f32)
        gate = jnp.where(bidx < qblk, gate, -jnp.inf)
        sel = jnp.zeros((nblk, TM), f32)
        for _ in range(MOBA_TOPK):
            m = jnp.max(gate, axis=0, keepdims=True)
            first = jnp.min(jnp.where(gate == m, bidx_f, float(nblk)), axis=0, keepdims=True)
            hit = jnp.logical_and(bidx_f == first, m > -jnp.inf)
            sel = jnp.where(hit, 1.0, sel)
            gate = jnp.where(hit, -jnp.inf, gate)
        notsel = jnp.where(bidx == qblk, 0.0, 1.0 - sel)
        pad = jnp.zeros((LANES - nblk, TM), f32)
        if nblk == HEAD_DIM:
            ns = jnp.concatenate([pad, notsel], axis=0).T
        else:
            ns = jnp.concatenate([pad[:HEAD_DIM], notsel, pad[HEAD_DIM:]], axis=0).T
        q_ref[h] = jnp.where(lane < HEAD_DIM, _head_lanes(za, h, 0) * SCALE, ns).astype(bf16)
        k_ref[h] = jnp.where(lane < HEAD_DIM, _head_lanes(za, h, 1),
                             jnp.where(lane == HEAD_DIM + kblk, NEG, 0.0)).astype(bf16)
        v_ref[h] = jnp.where(lane < HEAD_DIM, _head_lanes(za, h, 2),
                             jnp.where(lane == HEAD_DIM, 1.0, 0.0)).astype(bf16)


def _fox_operands(zb, fb, cum_ref, q_ref, k_ref, v_ref):
    logf = jnp.minimum(fb, 0.0) - jnp.log1p(jnp.exp(-jnp.abs(fb)))
    r = lax.broadcasted_iota(jnp.int32, (TM, TM), 0)
    c = lax.broadcasted_iota(jnp.int32, (TM, TM), 1)
    tri = jnp.where(c <= r, 1.0, 0.0).astype(f32)
    cum = jnp.dot(tri, logf, precision=_HI, preferred_element_type=f32) + cum_ref[...]
    cum_ref[...] = cum[TM - 1:TM, :]
    lane = lax.broadcasted_iota(jnp.int32, (TM, LANES), 1)
    d = HEAD_DIM
    for h in range(N_HEADS):
        cb = jnp.broadcast_to(cum[:, h:h + 1], (TM, LANES))
        hi = cb.astype(bf16).astype(f32)
        r1 = cb - hi
        mid = r1.astype(bf16).astype(f32)
        lo = r1 - mid
        qa = jnp.where(lane < d, _head_lanes(zb, h, 0) * SCALE,
             jnp.where(lane == d, hi,
             jnp.where(lane == d + 1, mid,
             jnp.where(lane == d + 2, lo,
             jnp.where(lane < d + 6, 1.0, 0.0)))))
        q_ref[h] = qa.astype(bf16)
        ka = jnp.where(lane < d, _head_lanes(zb, h, 1),
             jnp.where(lane < d + 3, 1.0,
             jnp.where(lane == d + 3, -hi,
             jnp.where(lane == d + 4, -mid,
             jnp.where(lane == d + 5, -lo, 0.0)))))
        k_ref[h] = ka.astype(bf16)
        v_ref[h] = jnp.where(lane < d, _head_lanes(zb, h, 2),
                             jnp.where(lane == d, 1.0, 0.0)).astype(bf16)


def _inproj_kernel(x_ref, g_ref, w_ref, bf_ref, qa_ref, ka_ref, va_ref, qb_ref, kb_ref, vb_ref, c_ref,
                   km_ref, cum_ref, *, nblk):
    i = pl.program_id(0)

    @pl.when(i == 0)
    def _():
        km_ref[...] = jnp.zeros_like(km_ref)
        cum_ref[...] = jnp.zeros_like(cum_ref)

    h = _rms(x_ref[...], g_ref[...]).astype(bf16)

    def mm(span):
        return jnp.dot(h, w_ref[:, span[0]:span[1]], preferred_element_type=f32)

    _moba_operands(i, mm(_QKV_A), km_ref, qa_ref, ka_ref, va_ref, nblk)
    _fox_operands(mm(_QKV_B), mm(_F_B) + bf_ref[...], cum_ref, qb_ref, kb_ref, vb_ref)
    c_ref[...] = mm(_C_ALL).astype(bf16)


def _inproj(x2, g, w1, bf_lane, layer):
    s = x2.shape[0]
    nblk = s // MOBA_BLOCK
    hb = pl.BlockSpec((N_HEADS, TM, LANES), lambda i: (0, i, 0))
    aug = jax.ShapeDtypeStruct((N_HEADS, s, LANES), bf16)
    return pl.pallas_call(
        functools.partial(_inproj_kernel, nblk=nblk),
        grid=(s // TM,),
        in_specs=[pl.BlockSpec((TM, D_MODEL), lambda i: (i, 0)),
                  pl.BlockSpec((1, D_MODEL), lambda i: (0, 0)),
                  _layer_weight(w1, layer),
                  pl.BlockSpec((1, LANES), lambda i: (0, 0))],
        out_specs=[hb] * 6 + [pl.BlockSpec((TM, _C_W), lambda i: (i, 0))],
        out_shape=[aug] * 6 + [jax.ShapeDtypeStruct((s, _C_W), bf16)],
        scratch_shapes=[pltpu.VMEM((nblk, N_HEADS * HEAD_DIM), f32), pltpu.VMEM((1, LANES), f32)],
        compiler_params=_params("arbitrary"),
        name="inproj",
    )(x2, g, w1, bf_lane)


def _flash_step(q, k_ref, v_ref, j, carry, add_bias):
    m, acc = carry
    off = pl.multiple_of(j * TA, TA)
    k = k_ref[0, pl.ds(off, TA), :]
    v = v_ref[0, pl.ds(off, TA), :]
    s = lax.dot_general(q, k, (((1,), (1,)), ((), ())), preferred_element_type=f32)
    s = add_bias(s)
    m_new = jnp.maximum(m, jnp.max(s, axis=-1, keepdims=True))
    alpha = jnp.exp(m - m_new)
    p = jnp.exp(s - m_new)
    acc = alpha * acc + jnp.dot(p.astype(bf16), v, preferred_element_type=f32)
    return m_new, acc


def _flash_loop(lo, hi, step, carry):
    trips = (hi - lo) // FLASH_UNROLL

    def body(t, c):
        for u in range(FLASH_UNROLL):
            c = step(lo + FLASH_UNROLL * t + u, c)
        return c

    carry = lax.fori_loop(0, trips, body, carry)
    return lax.fori_loop(lo + FLASH_UNROLL * trips, hi, step, carry)


def _flash_finish(o_ref, acc):
    o_ref[0] = (acc / acc[:, HEAD_DIM:HEAD_DIM + 1]).astype(o_ref.dtype)


def _flash_init():
    return jnp.full((TA, 1), -jnp.inf, f32), jnp.zeros((TA, LANES), f32)


def _fox_flash_kernel(q_ref, k_ref, v_ref, o_ref):
    i = pl.program_id(1)
    q = q_ref[0]
    carry = _flash_loop(
        0, i, lambda j, c: _flash_step(q, k_ref, v_ref, j, c, lambda s: s), _flash_init())
    row = lax.broadcasted_iota(jnp.int32, (TA, TA), 0)
    col = lax.broadcasted_iota(jnp.int32, (TA, TA), 1)
    _, acc = _flash_step(q, k_ref, v_ref, i, carry, lambda s: jnp.where(col <= row, s, NEG))
    _flash_finish(o_ref, acc)


def _moba_flash_kernel(far_ref, q_ref, k_ref, v_ref, bias_ref, o_ref, *, near):
    h = pl.program_id(0)
    i = pl.program_id(1)
    q = q_ref[0]
    far = far_ref[h]
    n_far = jnp.maximum(i - (near - 1), 0)
    carry = _flash_loop(
        0, n_far, lambda j, c: _flash_step(q, k_ref, v_ref, j, c, lambda s: s + far), _flash_init())
    carry = _flash_loop(
        n_far, i + 1,
        lambda j, c: _flash_step(q, k_ref, v_ref, j, c, lambda s: s + bias_ref[0, i - j]), carry)
    _flash_finish(o_ref, carry[1])


def _flash_specs(s):
    qspec = pl.BlockSpec((1, TA, LANES), lambda h, i: (h, i, 0))
    kvspec = _resident((1, s, LANES), lambda h, i: (h, 0, 0))
    return qspec, kvspec


def _fox_flash(q, k, v):
    s = q.shape[1]
    qspec, kvspec = _flash_specs(s)
    return pl.pallas_call(
        _fox_flash_kernel,
        grid=(N_HEADS, s // TA),
        in_specs=[qspec, kvspec, kvspec],
        out_specs=qspec,
        out_shape=jax.ShapeDtypeStruct((N_HEADS, s, LANES), bf16),
        compiler_params=_params("arbitrary", "arbitrary"),
        name="fox_flash",
    )(q, k, v)


def _moba_flash(far, q, k, v, bias):
    s = q.shape[1]
    near = bias.shape[1]
    qspec, kvspec = _flash_specs(s)
    return pl.pallas_call(
        functools.partial(_moba_flash_kernel, near=near),
        grid=(N_HEADS, s // TA),
        in_specs=[pl.BlockSpec(memory_space=pltpu.SMEM), qspec, kvspec, kvspec,
                  _resident((1, near, TA, TA), lambda h, i: (h, 0, 0, 0))],
        out_specs=qspec,
        out_shape=jax.ShapeDtypeStruct((N_HEADS, s, LANES), bf16),
        compiler_params=_params("arbitrary", "arbitrary"),
        name="moba_flash",
    )(far, q, k, v, bias)


def _swa_kernel(cur_ref, prev_ref, bias_ref, o_ref):
    i = pl.program_id(0)
    first = jnp.where(i == 0, 1, 0)
    lane = lax.broadcasted_iota(jnp.int32, (Q_BLOCK, LANES), 1)
    qb = Q_BLOCK
    ones = jnp.ones((2 * qb, LANES), bf16)
    for b in range(TM // qb):
        if b == 0:
            kband = jnp.concatenate([prev_ref[TM - qb:TM, _C_K:_C_V], cur_ref[0:qb, _C_K:_C_V]], axis=0)
            vband = jnp.concatenate([prev_ref[TM - qb:TM, _C_V:_C_W], cur_ref[0:qb, _C_V:_C_W]], axis=0)
            variant = first
        else:
            kband = cur_ref[qb * (b - 1):qb * (b + 1), _C_K:_C_V]
            vband = cur_ref[qb * (b - 1):qb * (b + 1), _C_V:_C_W]
            variant = 0
        kband = jnp.where(lax.broadcasted_iota(jnp.int32, kband.shape, 0) == 0, 0.0, kband).astype(bf16)
        vband = jnp.where(lax.broadcasted_iota(jnp.int32, vband.shape, 0) == 0, 0.0, vband).astype(bf16)
        outs = []
        for g in range(C_HEADS // C_GROUP):
            qs = jnp.concatenate(
                [cur_ref[qb * b:qb * (b + 1), LANES * (C_GROUP * g + u):LANES * (C_GROUP * g + u + 1)]
                 for u in range(C_GROUP)], axis=0)
            s = lax.dot_general(qs, kband, (((1,), (1,)), ((), ())), preferred_element_type=f32)
            s = s + bias_ref[variant, g]
            p = jnp.exp(s - jnp.max(s, axis=-1, keepdims=True)).astype(bf16)
            den = jnp.dot(p, ones, preferred_element_type=f32)
            for u in range(C_GROUP):
                half = (C_GROUP * g + u) % 2
                vg = vband[:, 0:LANES] if g == half else vband[:, LANES:2 * LANES]
                pv = jnp.dot(p[qb * u:qb * (u + 1)], vg, preferred_element_type=f32)
                outs.append(pv / den[qb * u:qb * (u + 1)])
        for pr in range(C_HEADS // 2):
            o_ref[qb * b:qb * (b + 1), LANES * pr:LANES * (pr + 1)] = jnp.where(
                lane < HEAD_DIM, outs[2 * pr], outs[2 * pr + 1]).astype(o_ref.dtype)


def _swa(c, bias, layer):
    s = c.shape[0]
    return pl.pallas_call(
        _swa_kernel,
        grid=(s // TM,),
        in_specs=[pl.BlockSpec((TM, _C_W), lambda i: (i, 0)),
                  pl.BlockSpec((TM, _C_W), lambda i: (jnp.maximum(i - 1, 0), 0)),
                  pl.BlockSpec((None,) + bias.shape[1:], lambda i: (layer, 0, 0, 0, 0))],
        out_specs=pl.BlockSpec((TM, 512), lambda i: (i, 0)),
        out_shape=jax.ShapeDtypeStruct((s, 512), bf16),
        compiler_params=_params("arbitrary"),
        name="swa",
    )(c, c, bias)


def _mix_ffn_kernel(x_ref, g1_ref, wg_ref, oa_ref, ob_ref, oc_ref, wpa_ref, wpb_ref, wpc_ref, wo_ref,
                    g2_ref, wup_ref, cw_ref, cb_ref, wdn_ref, gf_ref, out_ref,
                    mg_ref, halo_ref, u_ref, act_ref, *, final):
    @pl.when(pl.program_id(0) == 0)
    def _():
        halo_ref[...] = jnp.zeros_like(halo_ref)

    x = x_ref[...]
    h = _rms(x, g1_ref[...]).astype(bf16)
    oc = oc_ref[...]
    cw = 512
    for n in range(D_MODEL // cw):
        lo = cw * n

        def heads(o_ref, w_ref):
            acc = jnp.dot(o_ref[0], w_ref[0:LANES, lo:lo + cw], preferred_element_type=f32)
            for hh in range(1, N_HEADS):
                acc += jnp.dot(o_ref[hh], w_ref[LANES * hh:LANES * (hh + 1), lo:lo + cw],
                               preferred_element_type=f32)
            return acc

        def gate(which):
            z = jnp.dot(h, wg_ref[:, D_MODEL * which + lo:D_MODEL * which + lo + cw],
                        preferred_element_type=f32)
            return jax.nn.sigmoid(z)

        mg = gate(0) * heads(oa_ref, wpa_ref)
        mg += gate(1) * heads(ob_ref, wpb_ref)
        mg += gate(2) * jnp.dot(oc, wpc_ref[:, lo:lo + cw], preferred_element_type=f32)
        mg_ref[:, lo:lo + cw] = mg.astype(bf16)
    x = x + jnp.dot(mg_ref[...], wo_ref[...], preferred_element_type=f32)

    h2 = _rms(jnp.concatenate([halo_ref[...], x], axis=0), g2_ref[...]).astype(bf16)
    halo_ref[...] = x[TM - HALO:TM]

    def conv(lo):
        u_ref[...] = jnp.dot(h2, wup_ref[:, lo:lo + FF_CHUNK], preferred_element_type=f32)
        w = cw_ref[:, lo:lo + FF_CHUNK]
        return (w[2:3] * u_ref[HALO:HALO + TM] + w[1:2] * u_ref[HALO - 1:HALO - 1 + TM]
                + w[0:1] * u_ref[HALO - 2:HALO - 2 + TM] + cb_ref[:, lo:lo + FF_CHUNK])

    for c in range(D_FF // FF_CHUNK):
        a = conv(FF_CHUNK * c)
        b = conv(D_FF + FF_CHUNK * c)
        act_ref[:, FF_CHUNK * c:FF_CHUNK * (c + 1)] = (a * jax.nn.sigmoid(a) * b).astype(bf16)
    o = x + jnp.dot(act_ref[...], wdn_ref[...], preferred_element_type=f32)
    if final:
        o = _rms(o, gf_ref[...])
    out_ref[...] = o


def _mix_ffn(x2, g1, wg, oa, ob, oc, wpa, wpb, wpc, wo, g2, wup, cw, cb, wdn, gf, layer):
    s = x2.shape[0]
    hb = pl.BlockSpec((N_HEADS, TM, LANES), lambda i: (0, i, 0))
    full = lambda a: _layer_weight(a, layer)
    vec = lambda a: pl.BlockSpec(a.shape, lambda i: (0,) * a.ndim)
    row = lambda w: pl.BlockSpec((TM, w), lambda i: (i, 0))
    return pl.pallas_call(
        functools.partial(_mix_ffn_kernel, final=(layer == DEPTH - 1)),
        grid=(s // TM,),
        in_specs=[row(D_MODEL), vec(g1), full(wg), hb, hb, row(512),
                  full(wpa), full(wpb), full(wpc), full(wo),
                  vec(g2), full(wup), vec(cw), vec(cb), full(wdn), vec(gf)],
        out_specs=row(D_MODEL),
        out_shape=jax.ShapeDtypeStruct((s, D_MODEL), f32),
        scratch_shapes=[pltpu.VMEM((TM, D_MODEL), bf16), pltpu.VMEM((HALO, D_MODEL), f32),
                        pltpu.VMEM((TM + HALO, FF_CHUNK), f32), pltpu.VMEM((TM, D_FF), bf16)],
        compiler_params=_params("arbitrary"),
        name="mix_ffn",
    )(x2, g1, wg, oa, ob, oc, wpa, wpb, wpc, wo, g2, wup, cw, cb, wdn, gf)


def _pad_head_rows(w, nh):
    l, _, d = w.shape
    w = jnp.pad(w.reshape(l, nh, HEAD_DIM, d), ((0, 0), (0, 0), (0, LANES - HEAD_DIM), (0, 0)))
    return w.reshape(l, nh * LANES, d)


def _fused_in_weight(w_in):
    l = w_in.shape[0]
    cuts = np.cumsum([0, 256, 256, 256, 256, 256, 256, N_HEADS, 512, 128, 128]).tolist()
    fb, qc, kc, vc = (w_in[..., cuts[n]:cuts[n + 1]] for n in range(6, 10))
    qc = (qc * SCALE).reshape(l, D_MODEL, C_HEADS, HEAD_DIM)
    zero = jnp.zeros_like(qc[:, :, 0])
    qc_pad = jnp.concatenate(
        [jnp.concatenate([qc[:, :, hq], zero] if hq < C_GROUP else [zero, qc[:, :, hq]], axis=-1)
         for hq in range(C_HEADS)], axis=-1)
    vc2 = jnp.concatenate([vc, vc[..., HEAD_DIM:], vc[..., :HEAD_DIM]], axis=-1)
    fbp = jnp.pad(fb, ((0, 0), (0, 0), (0, LANES - N_HEADS)))
    w1 = jnp.concatenate([w_in[..., :cuts[6]], qc_pad, kc, vc2, fbp], axis=-1)
    assert w1.shape[-1] == _W1_COLS
    return w1.astype(bf16), w_in[..., cuts[10]:].astype(bf16)


def _bias_lookup(tbl, dist):
    bucket = _t5_bucket(jnp.asarray(dist, jnp.int32))
    out = jnp.zeros((tbl.shape[0],) + bucket.shape, f32)
    for b in range(NUM_BUCKETS):
        col = tbl[:, b].astype(f32).reshape((-1,) + (1,) * bucket.ndim)
        out = jnp.where(bucket[None] == b, col, out)
    return out


def _toeplitz_kernel(u_ref, o_ref):
    x = jnp.broadcast_to(u_ref[0, 0], (TA, 2 * TA))
    o_ref[0, 0] = pltpu.roll(x, 0, 1, stride=1, stride_axis=0)[:, :TA]


def _moba_bias_tiles(tbl_a, s):
    near = min(-(-(MAX_DISTANCE + TA - 1) // TA), s // TA)
    m = np.arange(2 * TA)
    rel = np.where(m < TA, -m, np.where(m == TA, 0, 2 * TA - m))
    dist = np.arange(near)[:, None] * TA + rel[None, :]
    gen = jnp.where(dist >= 0, _bias_lookup(tbl_a, np.maximum(dist, 0)), NEG)
    tiles = pl.pallas_call(
        _toeplitz_kernel,
        grid=(N_HEADS, near),
        in_specs=[pl.BlockSpec((1, 1, 1, 2 * TA), lambda h, d: (h, d, 0, 0))],
        out_specs=pl.BlockSpec((1, 1, TA, TA), lambda h, d: (h, d, 0, 0)),
        out_shape=jax.ShapeDtypeStruct((N_HEADS, near, TA, TA), f32),
        compiler_params=_params("arbitrary", "arbitrary"),
        name="moba_bias_tiles",
    )(gen[:, :, None, :])
    far = _bias_lookup(tbl_a, np.asarray([near * TA]))[:, 0]
    return tiles, far


def _swa_tables(tbl_c, sinks):
    tl = np.arange(Q_BLOCK)
    sl = np.arange(2 * Q_BLOCK)
    dist = tl[:, None] + Q_BLOCK - sl[None, :]
    bias = _bias_lookup(tbl_c, dist)
    band = (dist >= 0) & (dist < WINDOW)
    assert not band[:, 0].any()
    masks = np.stack([band, band & (sl[None, :] >= Q_BLOCK)])
    table = jnp.where(masks[:, None], bias[None], NEG)
    sink = sinks.astype(f32)[:, None, :, None, None]
    table = jnp.where((sl == 0)[None, None, None, None, :], sink, table[None])
    return table.reshape(sinks.shape[0], 2, C_HEADS // C_GROUP, C_GROUP * Q_BLOCK, 2 * Q_BLOCK)


def kernel(x, ln1, w_in, b_f, sinks, w_pa, w_pb, w_pc, w_o, ln2, w_up, conv_w, conv_b, w_down,
           rel_bias, ln_f):
    bsz, s, _ = x.shape
    assert bsz == 1 and s % TM == 0 and s % TA == 0 and TM % MOBA_BLOCK == 0
    assert s // MOBA_BLOCK <= HEAD_DIM
    w1, wg = _fused_in_weight(w_in)
    wpa = _pad_head_rows(w_pa, N_HEADS).astype(bf16)
    wpb = _pad_head_rows(w_pb, N_HEADS).astype(bf16)
    wpc, wo, wup, wdn = (w.astype(bf16) for w in (w_pc, w_o, w_up, w_down))
    tbl_a = rel_bias[:, :N_HEADS].T
    tbl_c = rel_bias[:, N_HEADS:].T
    moba_tiles, moba_far = _moba_bias_tiles(tbl_a, s)
    swa_bias = _swa_tables(tbl_c, sinks)
    bf_lane = jnp.pad(b_f.astype(f32), ((0, 0), (0, LANES - N_HEADS)))[:, None, :]
    gf = ln_f.astype(f32)[None, :]

    x2 = x[0]
    for l in range(DEPTH):
        qa, ka, va, qb, kb, vb, c = _inproj(x2, ln1[l][None, :], w1, bf_lane[l], l)
        oa = _moba_flash(moba_far, qa, ka, va, moba_tiles)
        ob = _fox_flash(qb, kb, vb)
        oc = _swa(c, swa_bias, l)
        x2 = _mix_ffn(x2, ln1[l][None, :], wg, oa, ob, oc, wpa, wpb, wpc, wo,
                      ln2[l][None, :], wup, conv_w[l], conv_b[l][None, :], wdn, gf, l)
    return x2[None]
```

```python
import functools
import math

import numpy as np
import jax
import jax.numpy as jnp
from jax import lax
from jax.experimental import pallas as pl
from jax.experimental.pallas import tpu as pltpu

D_MODEL = 1024
DEPTH = 4
HEAD_DIM = 64
N_HEADS = 4
C_HEADS = 8
C_GROUP = 4
MOBA_BLOCK = 256
MOBA_TOPK = 3
Q_BLOCK = 128
WINDOW = 128
NUM_BUCKETS = 32
MAX_DISTANCE = 4096
D_FF = 2816
EPS = 1e-6
NEG = -1e30
SCALE = HEAD_DIM ** -0.5
LOG2E = math.log2(math.e)

LANES = 128
HALO = 8
TM = 512
TA = 1024
FLASH_UNROLL = 4
FF_CHUNK = 256
VMEM_LIMIT = 56 * 1024 * 1024

_QKV_A = (0, 768)
_QKV_B = (768, 1536)
_C_ALL = (1536, 2944)
_F_B = (2944, 3072)
_W1_COLS = 3072
_C_K = 1024
_C_V = 1152
_C_W = 1408

bf16 = jnp.bfloat16
f32 = jnp.float32
_HI = lax.Precision.HIGHEST


def _rms(x, g):
    ms = jnp.mean(x * x, axis=-1, keepdims=True)
    return x * lax.rsqrt(ms + EPS) * g


def _t5_bucket(dist):
    dist = jnp.maximum(dist, 0)
    max_exact = NUM_BUCKETS // 2
    d = jnp.maximum(dist.astype(jnp.float32), 1.0)
    large = max_exact + (jnp.log(d / max_exact) / math.log(MAX_DISTANCE / max_exact)
                         * (NUM_BUCKETS - max_exact)).astype(jnp.int32)
    large = jnp.minimum(large, NUM_BUCKETS - 1)
    return jnp.where(dist < max_exact, dist, large)


def _params(*sem):
    return pltpu.CompilerParams(dimension_semantics=sem, vmem_limit_bytes=VMEM_LIMIT)


def _resident(shape, index_map):
    return pl.BlockSpec(shape, index_map, pipeline_mode=pl.Buffered(1))


def _layer_weight(w, layer):
    return _resident((None,) + w.shape[1:], lambda *_: (layer,) + (0,) * (w.ndim - 1))


def _head_lanes(z, h, part):
    lo = N_HEADS * HEAD_DIM * part + LANES * (h // 2)
    col = z[:, lo:lo + LANES]
    return pltpu.roll(col, HEAD_DIM, 1) if h % 2 else col


def _moba_operands(i, za, km_ref, q_ref, k_ref, v_ref, nblk):
    width = N_HEADS * HEAD_DIM
    lane = lax.broadcasted_iota(jnp.int32, (TM, LANES), 1)
    kblk = (i * TM + lax.broadcasted_iota(jnp.int32, (TM, LANES), 0)) >> 8
    bidx = lax.broadcasted_iota(jnp.int32, (nblk, TM), 0)
    bidx_f = bidx.astype(f32)
    qblk = (i * TM + lax.broadcasted_iota(jnp.int32, (nblk, TM), 1)) >> 8
    member = jnp.where(bidx == qblk, 1.0 / MOBA_BLOCK, 0.0).astype(f32)
    km_ref[...] += jnp.dot(member, za[:, width:2 * width], precision=_HI, preferred_element_type=f32)
    km = km_ref[...]
    km_head = lax.broadcasted_iota(jnp.int32, (nblk, width), 1) >> 6
    q_all = za[:, 0:width]
    for h in range(N_HEADS):
        gate = lax.dot_general(jnp.where(km_head == h, km, 0.0), q_all, (((1,), (1,)), ((), ())),
                               precision=_HI, preferred_element_type=f32)
        gate = jnp.where(bidx < qblk, gate, -jnp.inf)
        sel = jnp.zeros((nblk, TM), f32)
        for _ in range(MOBA_TOPK):
            m = jnp.max(gate, axis=0, keepdims=True)
            first = jnp.min(jnp.where(gate == m, bidx_f, float(nblk)), axis=0, keepdims=True)
            hit = jnp.logical_and(bidx_f == first, m > -jnp.inf)
            sel = jnp.where(hit, 1.0, sel)
            gate = jnp.where(hit, -jnp.inf, gate)
        notsel = jnp.where(bidx == qblk, 0.0, 1.0 - sel)
        pad = jnp.zeros((LANES - nblk, TM), f32)
        if nblk == HEAD_DIM:
            ns = jnp.concatenate([pad, notsel], axis=0).T
        else:
            ns = jnp.concatenate([pad[:HEAD_DIM], notsel, pad[HEAD_DIM:]], axis=0).T
        q_ref[h] = jnp.where(lane < HEAD_DIM, _head_lanes(za, h, 0) * (SCALE * LOG2E), ns).astype(bf16)
        k_ref[h] = jnp.where(lane < HEAD_DIM, _head_lanes(za, h, 1),
                             jnp.where(lane == HEAD_DIM + kblk, NEG, 0.0)).astype(bf16)
        v_ref[h] = jnp.where(lane < HEAD_DIM, _head_lanes(za, h, 2),
                             jnp.where(lane == HEAD_DIM, 1.0, 0.0)).astype(bf16)


def _fox_operands(zb, fb, cum_ref, q_ref, k_ref, v_ref):
    logf = jnp.minimum(fb, 0.0) - jnp.log1p(jnp.exp(-jnp.abs(fb)))
    r = lax.broadcasted_iota(jnp.int32, (TM, TM), 0)
    c = lax.broadcasted_iota(jnp.int32, (TM, TM), 1)
    tri = jnp.where(c <= r, 1.0, 0.0).astype(f32)
    cum = jnp.dot(tri, logf, precision=_HI, preferred_element_type=f32) + cum_ref[...]
    cum_ref[...] = cum[TM - 1:TM, :]
    lane = lax.broadcasted_iota(jnp.int32, (TM, LANES), 1)
    d = HEAD_DIM
    for h in range(N_HEADS):
        cb = jnp.broadcast_to(cum[:, h:h + 1], (TM, LANES)) * LOG2E
        hi = cb.astype(bf16).astype(f32)
        r1 = cb - hi
        mid = r1.astype(bf16).astype(f32)
        lo = r1 - mid
        qa = jnp.where(lane < d, _head_lanes(zb, h, 0) * (SCALE * LOG2E),
             jnp.where(lane == d, hi,
             jnp.where(lane == d + 1, mid,
             jnp.where(lane == d + 2, lo,
             jnp.where(lane < d + 6, 1.0, 0.0)))))
        q_ref[h] = qa.astype(bf16)
        ka = jnp.where(lane < d, _head_lanes(zb, h, 1),
             jnp.where(lane < d + 3, 1.0,
             jnp.where(lane == d + 3, -hi,
             jnp.where(lane == d + 4, -mid,
             jnp.where(lane == d + 5, -lo, 0.0)))))
        k_ref[h] = ka.astype(bf16)
        v_ref[h] = jnp.where(lane < d, _head_lanes(zb, h, 2),
                             jnp.where(lane == d, 1.0, 0.0)).astype(bf16)


def _inproj_kernel(x_ref, g_ref, w_ref, bf_ref, qa_ref, ka_ref, va_ref, qb_ref, kb_ref, vb_ref, c_ref,
                   km_ref, cum_ref, *, nblk):
    i = pl.program_id(0)

    @pl.when(i == 0)
    def _():
        km_ref[...] = jnp.zeros_like(km_ref)
        cum_ref[...] = jnp.zeros_like(cum_ref)

    h = _rms(x_ref[...], g_ref[...]).astype(bf16)

    def mm(span):
        return jnp.dot(h, w_ref[:, span[0]:span[1]], preferred_element_type=f32)

    _moba_operands(i, mm(_QKV_A), km_ref, qa_ref, ka_ref, va_ref, nblk)
    _fox_operands(mm(_QKV_B), mm(_F_B) + bf_ref[...], cum_ref, qb_ref, kb_ref, vb_ref)
    c_ref[...] = mm(_C_ALL).astype(bf16)


def _inproj(x2, g, w1, bf_lane, layer):
    s = x2.shape[0]
    nblk = s // MOBA_BLOCK
    hb = pl.BlockSpec((N_HEADS, TM, LANES), lambda i: (0, i, 0))
    aug = jax.ShapeDtypeStruct((N_HEADS, s, LANES), bf16)
    return pl.pallas_call(
        functools.partial(_inproj_kernel, nblk=nblk),
        grid=(s // TM,),
        in_specs=[pl.BlockSpec((TM, D_MODEL), lambda i: (i, 0)),
                  pl.BlockSpec((1, D_MODEL), lambda i: (0, 0)),
                  _layer_weight(w1, layer),
                  pl.BlockSpec((1, LANES), lambda i: (0, 0))],
        out_specs=[hb] * 6 + [pl.BlockSpec((TM, _C_W), lambda i: (i, 0))],
        out_shape=[aug] * 6 + [jax.ShapeDtypeStruct((s, _C_W), bf16)],
        scratch_shapes=[pltpu.VMEM((nblk, N_HEADS * HEAD_DIM), f32), pltpu.VMEM((1, LANES), f32)],
        compiler_params=_params("arbitrary"),
        name="inproj",
    )(x2, g, w1, bf_lane)


def _flash_step(q, k_ref, v_ref, j, carry, add_bias):
    m, acc = carry
    off = pl.multiple_of(j * TA, TA)
    k = k_ref[0, pl.ds(off, TA), :]
    v = v_ref[0, pl.ds(off, TA), :]
    s = lax.dot_general(q, k, (((1,), (1,)), ((), ())), preferred_element_type=f32)
    s = add_bias(s)
    m_new = jnp.maximum(m, jnp.max(s, axis=-1, keepdims=True))
    alpha = jnp.exp2(m - m_new)
    p = jnp.exp2(s - m_new)
    acc = alpha * acc + jnp.dot(p.astype(bf16), v, preferred_element_type=f32)
    return m_new, acc


def _flash_loop(lo, hi, step, carry):
    trips = (hi - lo) // FLASH_UNROLL

    def body(t, c):
        for u in range(FLASH_UNROLL):
            c = step(lo + FLASH_UNROLL * t + u, c)
        return c

    carry = lax.fori_loop(0, trips, body, carry)
    return lax.fori_loop(lo + FLASH_UNROLL * trips, hi, step, carry)


def _flash_finish(o_ref, acc):
    o_ref[0] = (acc / acc[:, HEAD_DIM:HEAD_DIM + 1]).astype(o_ref.dtype)


def _flash_init():
    return jnp.full((TA, 1), -jnp.inf, f32), jnp.zeros((TA, LANES), f32)


def _fox_flash_kernel(q_ref, k_ref, v_ref, o_ref):
    i = pl.program_id(1)
    q = q_ref[0]
    carry = _flash_loop(
        0, i, lambda j, c: _flash_step(q, k_ref, v_ref, j, c, lambda s: s), _flash_init())
    row = lax.broadcasted_iota(jnp.int32, (TA, TA), 0)
    col = lax.broadcasted_iota(jnp.int32, (TA, TA), 1)
    _, acc = _flash_step(q, k_ref, v_ref, i, carry, lambda s: jnp.where(col <= row, s, NEG))
    _flash_finish(o_ref, acc)


def _moba_flash_kernel(far_ref, q_ref, k_ref, v_ref, bias_ref, o_ref, *, near):
    h = pl.program_id(0)
    i = pl.program_id(1)
    q = q_ref[0]
    far = far_ref[h]
    n_far = jnp.maximum(i - (near - 1), 0)
    carry = _flash_loop(
        0, n_far, lambda j, c: _flash_step(q, k_ref, v_ref, j, c, lambda s: s + far), _flash_init())
    carry = _flash_loop(
        n_far, i + 1,
        lambda j, c: _flash_step(q, k_ref, v_ref, j, c, lambda s: s + bias_ref[0, i - j]), carry)
    _flash_finish(o_ref, carry[1])


def _flash_specs(s):
    qspec = pl.BlockSpec((1, TA, LANES), lambda h, i: (h, i, 0))
    kvspec = _resident((1, s, LANES), lambda h, i: (h, 0, 0))
    return qspec, kvspec


def _fox_flash(q, k, v):
    s = q.shape[1]
    qspec, kvspec = _flash_specs(s)
    return pl.pallas_call(
        _fox_flash_kernel,
        grid=(N_HEADS, s // TA),
        in_specs=[qspec, kvspec, kvspec],
        out_specs=qspec,
        out_shape=jax.ShapeDtypeStruct((N_HEADS, s, LANES), bf16),
        compiler_params=_params("arbitrary", "arbitrary"),
        name="fox_flash",
    )(q, k, v)


def _moba_flash(far, q, k, v, bias):
    s = q.shape[1]
    near = bias.shape[1]
    qspec, kvspec = _flash_specs(s)
    return pl.pallas_call(
        functools.partial(_moba_flash_kernel, near=near),
        grid=(N_HEADS, s // TA),
        in_specs=[pl.BlockSpec(memory_space=pltpu.SMEM), qspec, kvspec, kvspec,
                  _resident((1, near, TA, TA), lambda h, i: (h, 0, 0, 0))],
        out_specs=qspec,
        out_shape=jax.ShapeDtypeStruct((N_HEADS, s, LANES), bf16),
        compiler_params=_params("arbitrary", "arbitrary"),
        name="moba_flash",
    )(far, q, k, v, bias)


def _swa_kernel(cur_ref, prev_ref, bias_ref, o_ref):
    i = pl.program_id(0)
    first = jnp.where(i == 0, 1, 0)
    lane = lax.broadcasted_iota(jnp.int32, (Q_BLOCK, LANES), 1)
    qb = Q_BLOCK
    ones = jnp.ones((2 * qb, LANES), bf16)
    for b in range(TM // qb):
        if b == 0:
            kband = jnp.concatenate([prev_ref[TM - qb:TM, _C_K:_C_V], cur_ref[0:qb, _C_K:_C_V]], axis=0)
            vband = jnp.concatenate([prev_ref[TM - qb:TM, _C_V:_C_W], cur_ref[0:qb, _C_V:_C_W]], axis=0)
            variant = first
        else:
            kband = cur_ref[qb * (b - 1):qb * (b + 1), _C_K:_C_V]
            vband = cur_ref[qb * (b - 1):qb * (b + 1), _C_V:_C_W]
            variant = 0
        kband = jnp.where(lax.broadcasted_iota(jnp.int32, kband.shape, 0) == 0, 0.0, kband).astype(bf16)
        vband = jnp.where(lax.broadcasted_iota(jnp.int32, vband.shape, 0) == 0, 0.0, vband).astype(bf16)
        outs = []
        for g in range(C_HEADS // C_GROUP):
            qs = jnp.concatenate(
                [cur_ref[qb * b:qb * (b + 1), LANES * (C_GROUP * g + u):LANES * (C_GROUP * g + u + 1)]
                 for u in range(C_GROUP)], axis=0)
            s = lax.dot_general(qs, kband, (((1,), (1,)), ((), ())), preferred_element_type=f32)
            s = s + bias_ref[variant, g]
            p = jnp.exp(s - jnp.max(s, axis=-1, keepdims=True)).astype(bf16)
            den = jnp.dot(p, ones, preferred_element_type=f32)
            for u in range(C_GROUP):
                half = (C_GROUP * g + u) % 2
                vg = vband[:, 0:LANES] if g == half else vband[:, LANES:2 * LANES]
                pv = jnp.dot(p[qb * u:qb * (u + 1)], vg, preferred_element_type=f32)
                outs.append(pv / den[qb * u:qb * (u + 1)])
        for pr in range(C_HEADS // 2):
            o_ref[qb * b:qb * (b + 1), LANES * pr:LANES * (pr + 1)] = jnp.where(
                lane < HEAD_DIM, outs[2 * pr], outs[2 * pr + 1]).astype(o_ref.dtype)


def _swa(c, bias, layer):
    s = c.shape[0]
    return pl.pallas_call(
        _swa_kernel,
        grid=(s // TM,),
        in_specs=[pl.BlockSpec((TM, _C_W), lambda i: (i, 0)),
                  pl.BlockSpec((TM, _C_W), lambda i: (jnp.maximum(i - 1, 0), 0)),
                  pl.BlockSpec((None,) + bias.shape[1:], lambda i: (layer, 0, 0, 0, 0))],
        out_specs=pl.BlockSpec((TM, 512), lambda i: (i, 0)),
        out_shape=jax.ShapeDtypeStruct((s, 512), bf16),
        compiler_params=_params("arbitrary"),
        name="swa",
    )(c, c, bias)


def _mix_ffn_kernel(x_ref, g1_ref, wg_ref, oa_ref, ob_ref, oc_ref, wpa_ref, wpb_ref, wpc_ref, wo_ref,
                    g2_ref, wup_ref, cw_ref, cb_ref, wdn_ref, gf_ref, out_ref,
                    mg_ref, halo_ref, u_ref, act_ref, *, final):
    @pl.when(pl.program_id(0) == 0)
    def _():
        halo_ref[...] = jnp.zeros_like(halo_ref)

    x = x_ref[...]
    h = _rms(x, g1_ref[...]).astype(bf16)
    oc = oc_ref[...]
    lane = lax.broadcasted_iota(jnp.int32, (TM, LANES), 1)

    def head_pairs(o_ref):
        return [jnp.where(lane < HEAD_DIM, o_ref[2 * a].astype(f32),
                          pltpu.roll(o_ref[2 * a + 1].astype(f32), HEAD_DIM, 1)).astype(bf16)
                for a in range(N_HEADS // 2)]

    oa, ob = head_pairs(oa_ref), head_pairs(ob_ref)
    cw = 512
    for n in range(D_MODEL // cw):
        lo = cw * n

        def heads(pairs, w_ref):
            acc = jnp.dot(pairs[0], w_ref[0:LANES, lo:lo + cw], preferred_element_type=f32)
            for a in range(1, N_HEADS // 2):
                acc += jnp.dot(pairs[a], w_ref[LANES * a:LANES * (a + 1), lo:lo + cw],
                               preferred_element_type=f32)
            return acc

        def gate(which):
            z = jnp.dot(h, wg_ref[:, D_MODEL * which + lo:D_MODEL * which + lo + cw],
                        preferred_element_type=f32)
            return jax.nn.sigmoid(z)

        mg = gate(0) * heads(oa, wpa_ref)
        mg += gate(1) * heads(ob, wpb_ref)
        mg += gate(2) * jnp.dot(oc, wpc_ref[:, lo:lo + cw], preferred_element_type=f32)
        mg_ref[:, lo:lo + cw] = mg.astype(bf16)
    x = x + jnp.dot(mg_ref[...], wo_ref[...], preferred_element_type=f32)

    h2 = _rms(jnp.concatenate([halo_ref[...], x], axis=0), g2_ref[...]).astype(bf16)
    halo_ref[...] = x[TM - HALO:TM]

    def conv(lo):
        u_ref[...] = jnp.dot(h2, wup_ref[:, lo:lo + FF_CHUNK], preferred_element_type=f32)
        w = cw_ref[:, lo:lo + FF_CHUNK]
        return (w[2:3] * u_ref[HALO:HALO + TM] + w[1:2] * u_ref[HALO - 1:HALO - 1 + TM]
                + w[0:1] * u_ref[HALO - 2:HALO - 2 + TM] + cb_ref[:, lo:lo + FF_CHUNK])

    for c in range(D_FF // FF_CHUNK):
        a = conv(FF_CHUNK * c)
        b = conv(D_FF + FF_CHUNK * c)
        act_ref[:, FF_CHUNK * c:FF_CHUNK * (c + 1)] = (a * jax.nn.sigmoid(a) * b).astype(bf16)
    o = x + jnp.dot(act_ref[...], wdn_ref[...], preferred_element_type=f32)
    if final:
        o = _rms(o, gf_ref[...])
    out_ref[...] = o


def _mix_ffn(x2, g1, wg, oa, ob, oc, wpa, wpb, wpc, wo, g2, wup, cw, cb, wdn, gf, layer):
    s = x2.shape[0]
    hb = pl.BlockSpec((N_HEADS, TM, LANES), lambda i: (0, i, 0))
    full = lambda a: _layer_weight(a, layer)
    vec = lambda a: pl.BlockSpec(a.shape, lambda i: (0,) * a.ndim)
    row = lambda w: pl.BlockSpec((TM, w), lambda i: (i, 0))
    return pl.pallas_call(
        functools.partial(_mix_ffn_kernel, final=(layer == DEPTH - 1)),
        grid=(s // TM,),
        in_specs=[row(D_MODEL), vec(g1), full(wg), hb, hb, row(512),
                  full(wpa), full(wpb), full(wpc), full(wo),
                  vec(g2), full(wup), vec(cw), vec(cb), full(wdn), vec(gf)],
        out_specs=row(D_MODEL),
        out_shape=jax.ShapeDtypeStruct((s, D_MODEL), f32),
        scratch_shapes=[pltpu.VMEM((TM, D_MODEL), bf16), pltpu.VMEM((HALO, D_MODEL), f32),
                        pltpu.VMEM((TM + HALO, FF_CHUNK), f32), pltpu.VMEM((TM, D_FF), bf16)],
        compiler_params=_params("arbitrary"),
        name="mix_ffn",
    )(x2, g1, wg, oa, ob, oc, wpa, wpb, wpc, wo, g2, wup, cw, cb, wdn, gf)


def _fused_in_weight(w_in):
    l = w_in.shape[0]
    cuts = np.cumsum([0, 256, 256, 256, 256, 256, 256, N_HEADS, 512, 128, 128]).tolist()
    fb, qc, kc, vc = (w_in[..., cuts[n]:cuts[n + 1]] for n in range(6, 10))
    qc = (qc * SCALE).reshape(l, D_MODEL, C_HEADS, HEAD_DIM)
    zero = jnp.zeros_like(qc[:, :, 0])
    qc_pad = jnp.concatenate(
        [jnp.concatenate([qc[:, :, hq], zero] if hq < C_GROUP else [zero, qc[:, :, hq]], axis=-1)
         for hq in range(C_HEADS)], axis=-1)
    vc2 = jnp.concatenate([vc, vc[..., HEAD_DIM:], vc[..., :HEAD_DIM]], axis=-1)
    fbp = jnp.pad(fb, ((0, 0), (0, 0), (0, LANES - N_HEADS)))
    w1 = jnp.concatenate([w_in[..., :cuts[6]], qc_pad, kc, vc2, fbp], axis=-1)
    assert w1.shape[-1] == _W1_COLS
    return w1.astype(bf16), w_in[..., cuts[10]:].astype(bf16)


def _bias_lookup(tbl, dist):
    bucket = _t5_bucket(jnp.asarray(dist, jnp.int32))
    out = jnp.zeros((tbl.shape[0],) + bucket.shape, f32)
    for b in range(NUM_BUCKETS):
        col = tbl[:, b].astype(f32).reshape((-1,) + (1,) * bucket.ndim)
        out = jnp.where(bucket[None] == b, col, out)
    return out


def _toeplitz_kernel(u_ref, o_ref):
    x = jnp.broadcast_to(u_ref[0, 0], (TA, 2 * TA))
    o_ref[0, 0] = pltpu.roll(x, 0, 1, stride=1, stride_axis=0)[:, :TA]


def _moba_bias_tiles(tbl_a, s):
    near = min(-(-(MAX_DISTANCE + TA - 1) // TA), s // TA)
    m = np.arange(2 * TA)
    rel = np.where(m < TA, -m, np.where(m == TA, 0, 2 * TA - m))
    dist = np.arange(near)[:, None] * TA + rel[None, :]
    gen = jnp.where(dist >= 0, _bias_lookup(tbl_a, np.maximum(dist, 0)) * LOG2E, NEG)
    tiles = pl.pallas_call(
        _toeplitz_kernel,
        grid=(N_HEADS, near),
        in_specs=[pl.BlockSpec((1, 1, 1, 2 * TA), lambda h, d: (h, d, 0, 0))],
        out_specs=pl.BlockSpec((1, 1, TA, TA), lambda h, d: (h, d, 0, 0)),
        out_shape=jax.ShapeDtypeStruct((N_HEADS, near, TA, TA), f32),
        compiler_params=_params("arbitrary", "arbitrary"),
        name="moba_bias_tiles",
    )(gen[:, :, None, :])
    far = _bias_lookup(tbl_a, np.asarray([near * TA]))[:, 0] * LOG2E
    return tiles, far


def _swa_tables(tbl_c, sinks):
    tl = np.arange(Q_BLOCK)
    sl = np.arange(2 * Q_BLOCK)
    dist = tl[:, None] + Q_BLOCK - sl[None, :]
    bias = _bias_lookup(tbl_c, dist)
    band = (dist >= 0) & (dist < WINDOW)
    assert not band[:, 0].any()
    masks = np.stack([band, band & (sl[None, :] >= Q_BLOCK)])
    table = jnp.where(masks[:, None], bias[None], NEG)
    sink = sinks.astype(f32)[:, None, :, None, None]
    table = jnp.where((sl == 0)[None, None, None, None, :], sink, table[None])
    return table.reshape(sinks.shape[0], 2, C_HEADS // C_GROUP, C_GROUP * Q_BLOCK, 2 * Q_BLOCK)


def kernel(x, ln1, w_in, b_f, sinks, w_pa, w_pb, w_pc, w_o, ln2, w_up, conv_w, conv_b, w_down,
           rel_bias, ln_f):
    bsz, s, _ = x.shape
    assert bsz == 1 and s % TM == 0 and s % TA == 0 and TM % MOBA_BLOCK == 0
    assert s // MOBA_BLOCK <= HEAD_DIM
    w1, wg = _fused_in_weight(w_in)
    wpa, wpb, wpc, wo, wup, wdn = (w.astype(bf16) for w in (w_pa, w_pb, w_pc, w_o, w_up, w_down))
    tbl_a = rel_bias[:, :N_HEADS].T
    tbl_c = rel_bias[:, N_HEADS:].T
    moba_tiles, moba_far = _moba_bias_tiles(tbl_a, s)
    swa_bias = _swa_tables(tbl_c, sinks)
    bf_lane = jnp.pad(b_f.astype(f32), ((0, 0), (0, LANES - N_HEADS)))[:, None, :]
    gf = ln_f.astype(f32)[None, :]

    x2 = x[0]
    for l in range(DEPTH):
        qa, ka, va, qb, kb, vb, c = _inproj(x2, ln1[l][None, :], w1, bf_lane[l], l)
        oa = _moba_flash(moba_far, qa, ka, va, moba_tiles)
        ob = _fox_flash(qb, kb, vb)
        oc = _swa(c, swa_bias, l)
        x2 = _mix_ffn(x2, ln1[l][None, :], wg, oa, ob, oc, wpa, wpb, wpc, wo,
                      ln2[l][None, :], wup, conv_w[l], conv_b[l][None, :], wdn, gf, l)
    return x2[None]
```

```python
import functools
import math

import numpy as np
import jax
import jax.numpy as jnp
from jax import lax
from jax.experimental import pallas as pl
from jax.experimental.pallas import tpu as pltpu

D_MODEL = 1024
DEPTH = 4
HEAD_DIM = 64
N_HEADS = 4
C_HEADS = 8
C_GROUP = 4
MOBA_BLOCK = 256
MOBA_TOPK = 3
Q_BLOCK = 128
WINDOW = 128
NUM_BUCKETS = 32
MAX_DISTANCE = 4096
D_FF = 2816
EPS = 1e-6
NEG = -1e30
SCALE = HEAD_DIM ** -0.5
LOG2E = math.log2(math.e)

LANES = 128
HALO = 8
TM = 512
TA = 1024
FLASH_UNROLL = 4
FF_CHUNK = 256
VMEM_LIMIT = 56 * 1024 * 1024

_QKV_A = (0, 768)
_QKV_B = (768, 1536)
_C_ALL = (1536, 2944)
_F_B = (2944, 3072)
_W1_COLS = 3072
_C_K = 1024
_C_V = 1152
_C_W = 1408

bf16 = jnp.bfloat16
f32 = jnp.float32
_HI = lax.Precision.HIGHEST


def _rms(x, g):
    ms = jnp.mean(x * x, axis=-1, keepdims=True)
    return x * lax.rsqrt(ms + EPS) * g


def _t5_bucket(dist):
    dist = jnp.maximum(dist, 0)
    max_exact = NUM_BUCKETS // 2
    d = jnp.maximum(dist.astype(jnp.float32), 1.0)
    large = max_exact + (jnp.log(d / max_exact) / math.log(MAX_DISTANCE / max_exact)
                         * (NUM_BUCKETS - max_exact)).astype(jnp.int32)
    large = jnp.minimum(large, NUM_BUCKETS - 1)
    return jnp.where(dist < max_exact, dist, large)


def _params(*sem):
    return pltpu.CompilerParams(dimension_semantics=sem, vmem_limit_bytes=VMEM_LIMIT)


def _resident(shape, index_map):
    return pl.BlockSpec(shape, index_map, pipeline_mode=pl.Buffered(1))


def _layer_weight(w, layer):
    return _resident((None,) + w.shape[1:], lambda *_: (layer,) + (0,) * (w.ndim - 1))


def _head_lanes(z, h, part):
    lo = N_HEADS * HEAD_DIM * part + LANES * (h // 2)
    col = z[:, lo:lo + LANES]
    return pltpu.roll(col, HEAD_DIM, 1) if h % 2 else col


def _moba_operands(i, za, km_ref, q_ref, k_ref, v_ref, nblk):
    width = N_HEADS * HEAD_DIM
    lane = lax.broadcasted_iota(jnp.int32, (TM, LANES), 1)
    kblk = (i * TM + lax.broadcasted_iota(jnp.int32, (TM, LANES), 0)) >> 8
    bidx = lax.broadcasted_iota(jnp.int32, (nblk, TM), 0)
    bidx_f = bidx.astype(f32)
    qblk = (i * TM + lax.broadcasted_iota(jnp.int32, (nblk, TM), 1)) >> 8
    member = jnp.where(bidx == qblk, 1.0 / MOBA_BLOCK, 0.0).astype(f32)
    km_ref[...] += jnp.dot(member, za[:, width:2 * width], precision=_HI, preferred_element_type=f32)
    km = km_ref[...]
    km_head = lax.broadcasted_iota(jnp.int32, (nblk, width), 1) >> 6
    q_all = za[:, 0:width]
    for h in range(N_HEADS):
        gate = lax.dot_general(jnp.where(km_head == h, km, 0.0), q_all, (((1,), (1,)), ((), ())),
                               precision=_HI, preferred_element_type=f32)
        gate = jnp.where(bidx < qblk, gate, -jnp.inf)
        sel = jnp.zeros((nblk, TM), f32)
        for _ in range(MOBA_TOPK):
            m = jnp.max(gate, axis=0, keepdims=True)
            first = jnp.min(jnp.where(gate == m, bidx_f, float(nblk)), axis=0, keepdims=True)
            hit = jnp.logical_and(bidx_f == first, m > -jnp.inf)
            sel = jnp.where(hit, 1.0, sel)
            gate = jnp.where(hit, -jnp.inf, gate)
        notsel = jnp.where(bidx == qblk, 0.0, 1.0 - sel)
        pad = jnp.zeros((LANES - nblk, TM), f32)
        if nblk == HEAD_DIM:
            ns = jnp.concatenate([pad, notsel], axis=0).T
        else:
            ns = jnp.concatenate([pad[:HEAD_DIM], notsel, pad[HEAD_DIM:]], axis=0).T
        q_ref[h] = jnp.where(lane < HEAD_DIM, _head_lanes(za, h, 0) * (SCALE * LOG2E), ns).astype(bf16)
        k_ref[h] = jnp.where(lane < HEAD_DIM, _head_lanes(za, h, 1),
                             jnp.where(lane == HEAD_DIM + kblk, NEG, 0.0)).astype(bf16)
        v_ref[h] = jnp.where(lane < HEAD_DIM, _head_lanes(za, h, 2),
                             jnp.where(lane == HEAD_DIM, 1.0, 0.0)).astype(bf16)


def _fox_operands(zb, fb, cum_ref, q_ref, k_ref, v_ref):
    logf = jnp.minimum(fb, 0.0) - jnp.log1p(jnp.exp(-jnp.abs(fb)))
    r = lax.broadcasted_iota(jnp.int32, (TM, TM), 0)
    c = lax.broadcasted_iota(jnp.int32, (TM, TM), 1)
    tri = jnp.where(c <= r, 1.0, 0.0).astype(f32)
    cum = jnp.dot(tri, logf, precision=_HI, preferred_element_type=f32) + cum_ref[...]
    cum_ref[...] = cum[TM - 1:TM, :]
    lane = lax.broadcasted_iota(jnp.int32, (TM, LANES), 1)
    d = HEAD_DIM
    for h in range(N_HEADS):
        cb = jnp.broadcast_to(cum[:, h:h + 1], (TM, LANES)) * LOG2E
        hi = cb.astype(bf16).astype(f32)
        r1 = cb - hi
        mid = r1.astype(bf16).astype(f32)
        lo = r1 - mid
        qa = jnp.where(lane < d, _head_lanes(zb, h, 0) * (SCALE * LOG2E),
             jnp.where(lane == d, hi,
             jnp.where(lane == d + 1, mid,
             jnp.where(lane == d + 2, lo,
             jnp.where(lane < d + 6, 1.0, 0.0)))))
        q_ref[h] = qa.astype(bf16)
        ka = jnp.where(lane < d, _head_lanes(zb, h, 1),
             jnp.where(lane < d + 3, 1.0,
             jnp.where(lane == d + 3, -hi,
             jnp.where(lane == d + 4, -mid,
             jnp.where(lane == d + 5, -lo, 0.0)))))
        k_ref[h] = ka.astype(bf16)
        v_ref[h] = jnp.where(lane < d, _head_lanes(zb, h, 2),
                             jnp.where(lane == d, 1.0, 0.0)).astype(bf16)


def _inproj_kernel(x_ref, g_ref, w_ref, bf_ref, qa_ref, ka_ref, va_ref, qb_ref, kb_ref, vb_ref, c_ref,
                   km_ref, cum_ref, za_ref, zb_ref, fb_ref, *, nblk):
    i = pl.program_id(0)

    @pl.when(i == 0)
    def _():
        km_ref[...] = jnp.zeros_like(km_ref)
        cum_ref[...] = jnp.zeros_like(cum_ref)

    h = _rms(x_ref[...], g_ref[...]).astype(bf16)

    def mm(span):
        return jnp.dot(h, w_ref[:, span[0]:span[1]], preferred_element_type=f32)

    za_ref[...] = mm(_QKV_A)
    zb_ref[...] = mm(_QKV_B)
    fb_ref[...] = mm(_F_B) + bf_ref[...]
    c_ref[...] = mm(_C_ALL).astype(bf16)
    _moba_operands(i, za_ref, km_ref, qa_ref, ka_ref, va_ref, nblk)
    _fox_operands(zb_ref, fb_ref[...], cum_ref, qb_ref, kb_ref, vb_ref)


def _inproj(x2, g, w1, bf_lane, layer):
    s = x2.shape[0]
    nblk = s // MOBA_BLOCK
    hb = pl.BlockSpec((N_HEADS, TM, LANES), lambda i: (0, i, 0))
    aug = jax.ShapeDtypeStruct((N_HEADS, s, LANES), bf16)
    return pl.pallas_call(
        functools.partial(_inproj_kernel, nblk=nblk),
        grid=(s // TM,),
        in_specs=[pl.BlockSpec((TM, D_MODEL), lambda i: (i, 0)),
                  pl.BlockSpec((1, D_MODEL), lambda i: (0, 0)),
                  _layer_weight(w1, layer),
                  pl.BlockSpec((1, LANES), lambda i: (0, 0))],
        out_specs=[hb] * 6 + [pl.BlockSpec((TM, _C_W), lambda i: (i, 0))],
        out_shape=[aug] * 6 + [jax.ShapeDtypeStruct((s, _C_W), bf16)],
        scratch_shapes=[pltpu.VMEM((nblk, N_HEADS * HEAD_DIM), f32), pltpu.VMEM((1, LANES), f32),
                        pltpu.VMEM((TM, _QKV_A[1] - _QKV_A[0]), f32),
                        pltpu.VMEM((TM, _QKV_B[1] - _QKV_B[0]), f32), pltpu.VMEM((TM, LANES), f32)],
        compiler_params=_params("arbitrary"),
        name="inproj",
    )(x2, g, w1, bf_lane)


def _flash_step(q, k_ref, v_ref, j, carry, add_bias):
    m, acc = carry
    off = pl.multiple_of(j * TA, TA)
    k = k_ref[0, pl.ds(off, TA), :]
    v = v_ref[0, pl.ds(off, TA), :]
    s = lax.dot_general(q, k, (((1,), (1,)), ((), ())), preferred_element_type=f32)
    s = add_bias(s)
    m_new = jnp.maximum(m, jnp.max(s, axis=-1, keepdims=True))
    alpha = jnp.exp2(m - m_new)
    p = jnp.exp2(s - m_new)
    acc = alpha * acc + jnp.dot(p.astype(bf16), v, preferred_element_type=f32)
    return m_new, acc


def _flash_loop(lo, hi, step, carry):
    for width in (FLASH_UNROLL, 2, 1):
        trips = (hi - lo) // width

        def body(t, c, width=width, lo=lo):
            for u in range(width):
                c = step(lo + width * t + u, c)
            return c

        carry = lax.fori_loop(0, trips, body, carry)
        lo = lo + width * trips
    return carry


def _flash_finish(o_ref, acc):
    o_ref[0] = (acc / acc[:, HEAD_DIM:HEAD_DIM + 1]).astype(o_ref.dtype)


def _flash_init():
    return jnp.full((TA, 1), -jnp.inf, f32), jnp.zeros((TA, LANES), f32)


def _fox_flash_kernel(q_ref, k_ref, v_ref, o_ref):
    i = pl.program_id(1)
    q = q_ref[0]
    carry = _flash_loop(
        0, i, lambda j, c: _flash_step(q, k_ref, v_ref, j, c, lambda s: s), _flash_init())
    row = lax.broadcasted_iota(jnp.int32, (TA, TA), 0)
    col = lax.broadcasted_iota(jnp.int32, (TA, TA), 1)
    _, acc = _flash_step(q, k_ref, v_ref, i, carry, lambda s: jnp.where(col <= row, s, NEG))
    _flash_finish(o_ref, acc)


def _moba_flash_kernel(far_ref, q_ref, k_ref, v_ref, bias_ref, o_ref, *, near):
    h = pl.program_id(0)
    i = pl.program_id(1)
    q = q_ref[0]
    far = far_ref[h]
    n_far = jnp.maximum(i - (near - 1), 0)
    carry = _flash_loop(
        0, n_far, lambda j, c: _flash_step(q, k_ref, v_ref, j, c, lambda s: s + far), _flash_init())
    carry = _flash_loop(
        n_far, i + 1,
        lambda j, c: _flash_step(q, k_ref, v_ref, j, c, lambda s: s + bias_ref[0, i - j]), carry)
    _flash_finish(o_ref, carry[1])


def _flash_specs(s):
    qspec = pl.BlockSpec((1, TA, LANES), lambda h, i: (h, i, 0))
    kvspec = _resident((1, s, LANES), lambda h, i: (h, 0, 0))
    return qspec, kvspec


def _fox_flash(q, k, v):
    s = q.shape[1]
    qspec, kvspec = _flash_specs(s)
    return pl.pallas_call(
        _fox_flash_kernel,
        grid=(N_HEADS, s // TA),
        in_specs=[qspec, kvspec, kvspec],
        out_specs=qspec,
        out_shape=jax.ShapeDtypeStruct((N_HEADS, s, LANES), bf16),
        compiler_params=_params("arbitrary", "arbitrary"),
        name="fox_flash",
    )(q, k, v)


def _moba_flash(far, q, k, v, bias):
    s = q.shape[1]
    near = bias.shape[1]
    qspec, kvspec = _flash_specs(s)
    return pl.pallas_call(
        functools.partial(_moba_flash_kernel, near=near),
        grid=(N_HEADS, s // TA),
        in_specs=[pl.BlockSpec(memory_space=pltpu.SMEM), qspec, kvspec, kvspec,
                  _resident((1, near, TA, TA), lambda h, i: (h, 0, 0, 0))],
        out_specs=qspec,
        out_shape=jax.ShapeDtypeStruct((N_HEADS, s, LANES), bf16),
        compiler_params=_params("arbitrary", "arbitrary"),
        name="moba_flash",
    )(far, q, k, v, bias)


def _swa_kernel(cur_ref, prev_ref, bias_ref, o_ref):
    i = pl.program_id(0)
    first = jnp.where(i == 0, 1, 0)
    lane = lax.broadcasted_iota(jnp.int32, (Q_BLOCK, LANES), 1)
    qb = Q_BLOCK
    ones = jnp.ones((2 * qb, LANES), bf16)
    for b in range(TM // qb):
        if b == 0:
            kband = jnp.concatenate([prev_ref[TM - qb:TM, _C_K:_C_V], cur_ref[0:qb, _C_K:_C_V]], axis=0)
            vband = jnp.concatenate([prev_ref[TM - qb:TM, _C_V:_C_W], cur_ref[0:qb, _C_V:_C_W]], axis=0)
            variant = first
        else:
            kband = cur_ref[qb * (b - 1):qb * (b + 1), _C_K:_C_V]
            vband = cur_ref[qb * (b - 1):qb * (b + 1), _C_V:_C_W]
            variant = 0
        kband = jnp.where(lax.broadcasted_iota(jnp.int32, kband.shape, 0) == 0, 0.0, kband).astype(bf16)
        vband = jnp.where(lax.broadcasted_iota(jnp.int32, vband.shape, 0) == 0, 0.0, vband).astype(bf16)
        outs = []
        for g in range(C_HEADS // C_GROUP):
            qs = jnp.concatenate(
                [cur_ref[qb * b:qb * (b + 1), LANES * (C_GROUP * g + u):LANES * (C_GROUP * g + u + 1)]
                 for u in range(C_GROUP)], axis=0)
            s = lax.dot_general(qs, kband, (((1,), (1,)), ((), ())), preferred_element_type=f32)
            s = s + bias_ref[variant, g]
            p = jnp.exp(s - jnp.max(s, axis=-1, keepdims=True)).astype(bf16)
            den = jnp.dot(p, ones, preferred_element_type=f32)
            for u in range(C_GROUP):
                half = (C_GROUP * g + u) % 2
                vg = vband[:, 0:LANES] if g == half else vband[:, LANES:2 * LANES]
                pv = jnp.dot(p[qb * u:qb * (u + 1)], vg, preferred_element_type=f32)
                outs.append(pv / den[qb * u:qb * (u + 1)])
        for pr in range(C_HEADS // 2):
            o_ref[qb * b:qb * (b + 1), LANES * pr:LANES * (pr + 1)] = jnp.where(
                lane < HEAD_DIM, outs[2 * pr], outs[2 * pr + 1]).astype(o_ref.dtype)


def _swa(c, bias, layer):
    s = c.shape[0]
    return pl.pallas_call(
        _swa_kernel,
        grid=(s // TM,),
        in_specs=[pl.BlockSpec((TM, _C_W), lambda i: (i, 0)),
                  pl.BlockSpec((TM, _C_W), lambda i: (jnp.maximum(i - 1, 0), 0)),
                  pl.BlockSpec((None,) + bias.shape[1:], lambda i: (layer, 0, 0, 0, 0))],
        out_specs=pl.BlockSpec((TM, 512), lambda i: (i, 0)),
        out_shape=jax.ShapeDtypeStruct((s, 512), bf16),
        compiler_params=_params("arbitrary"),
        name="swa",
    )(c, c, bias)


def _mix_ffn_kernel(x_ref, g1_ref, wg_ref, oa_ref, ob_ref, oc_ref, wpa_ref, wpb_ref, wpc_ref, wo_ref,
                    g2_ref, wup_ref, cw_ref, cb_ref, wdn_ref, gf_ref, out_ref,
                    mg_ref, halo_ref, u_ref, act_ref, *, final):
    @pl.when(pl.program_id(0) == 0)
    def _():
        halo_ref[...] = jnp.zeros_like(halo_ref)

    x = x_ref[...]
    h = _rms(x, g1_ref[...]).astype(bf16)
    oc = oc_ref[...]
    lane = lax.broadcasted_iota(jnp.int32, (TM, LANES), 1)

    def head_pairs(o_ref):
        return [jnp.where(lane < HEAD_DIM, o_ref[2 * a].astype(f32),
                          pltpu.roll(o_ref[2 * a + 1].astype(f32), HEAD_DIM, 1)).astype(bf16)
                for a in range(N_HEADS // 2)]

    oa, ob = head_pairs(oa_ref), head_pairs(ob_ref)
    cw = 512
    for n in range(D_MODEL // cw):
        lo = cw * n

        def heads(pairs, w_ref):
            acc = jnp.dot(pairs[0], w_ref[0:LANES, lo:lo + cw], preferred_element_type=f32)
            for a in range(1, N_HEADS // 2):
                acc += jnp.dot(pairs[a], w_ref[LANES * a:LANES * (a + 1), lo:lo + cw],
                               preferred_element_type=f32)
            return acc

        def gate(which):
            z = jnp.dot(h, wg_ref[:, D_MODEL * which + lo:D_MODEL * which + lo + cw],
                        preferred_element_type=f32)
            return jax.nn.sigmoid(z)

        mg = gate(0) * heads(oa, wpa_ref)
        mg += gate(1) * heads(ob, wpb_ref)
        mg += gate(2) * jnp.dot(oc, wpc_ref[:, lo:lo + cw], preferred_element_type=f32)
        mg_ref[:, lo:lo + cw] = mg.astype(bf16)
    x = x + jnp.dot(mg_ref[...], wo_ref[...], preferred_element_type=f32)

    h2 = _rms(jnp.concatenate([halo_ref[...], x], axis=0), g2_ref[...]).astype(bf16)
    halo_ref[...] = x[TM - HALO:TM]

    def conv(lo):
        u_ref[...] = jnp.dot(h2, wup_ref[:, lo:lo + FF_CHUNK], preferred_element_type=f32)
        w = cw_ref[:, lo:lo + FF_CHUNK]
        return (w[2:3] * u_ref[HALO:HALO + TM] + w[1:2] * u_ref[HALO - 1:HALO - 1 + TM]
                + w[0:1] * u_ref[HALO - 2:HALO - 2 + TM] + cb_ref[:, lo:lo + FF_CHUNK])

    for c in range(D_FF // FF_CHUNK):
        a = conv(FF_CHUNK * c)
        b = conv(D_FF + FF_CHUNK * c)
        act_ref[:, FF_CHUNK * c:FF_CHUNK * (c + 1)] = (a * jax.nn.sigmoid(a) * b).astype(bf16)
    o = x + jnp.dot(act_ref[...], wdn_ref[...], preferred_element_type=f32)
    if final:
        o = _rms(o, gf_ref[...])
    out_ref[...] = o


def _mix_ffn(x2, g1, wg, oa, ob, oc, wpa, wpb, wpc, wo, g2, wup, cw, cb, wdn, gf, layer):
    s = x2.shape[0]
    hb = pl.BlockSpec((N_HEADS, TM, LANES), lambda i: (0, i, 0))
    full = lambda a: _layer_weight(a, layer)
    vec = lambda a: pl.BlockSpec(a.shape, lambda i: (0,) * a.ndim)
    row = lambda w: pl.BlockSpec((TM, w), lambda i: (i, 0))
    return pl.pallas_call(
        functools.partial(_mix_ffn_kernel, final=(layer == DEPTH - 1)),
        grid=(s // TM,),
        in_specs=[row(D_MODEL), vec(g1), full(wg), hb, hb, row(512),
                  full(wpa), full(wpb), full(wpc), full(wo),
                  vec(g2), full(wup), vec(cw), vec(cb), full(wdn), vec(gf)],
        out_specs=row(D_MODEL),
        out_shape=jax.ShapeDtypeStruct((s, D_MODEL), f32),
        scratch_shapes=[pltpu.VMEM((TM, D_MODEL), bf16), pltpu.VMEM((HALO, D_MODEL), f32),
                        pltpu.VMEM((TM + HALO, FF_CHUNK), f32), pltpu.VMEM((TM, D_FF), bf16)],
        compiler_params=_params("arbitrary"),
        name="mix_ffn",
    )(x2, g1, wg, oa, ob, oc, wpa, wpb, wpc, wo, g2, wup, cw, cb, wdn, gf)


def _fused_in_weight(w_in):
    l = w_in.shape[0]
    cuts = np.cumsum([0, 256, 256, 256, 256, 256, 256, N_HEADS, 512, 128, 128]).tolist()
    fb, qc, kc, vc = (w_in[..., cuts[n]:cuts[n + 1]] for n in range(6, 10))
    qc = (qc * SCALE).reshape(l, D_MODEL, C_HEADS, HEAD_DIM)
    zero = jnp.zeros_like(qc[:, :, 0])
    qc_pad = jnp.concatenate(
        [jnp.concatenate([qc[:, :, hq], zero] if hq < C_GROUP else [zero, qc[:, :, hq]], axis=-1)
         for hq in range(C_HEADS)], axis=-1)
    vc2 = jnp.concatenate([vc, vc[..., HEAD_DIM:], vc[..., :HEAD_DIM]], axis=-1)
    fbp = jnp.pad(fb, ((0, 0), (0, 0), (0, LANES - N_HEADS)))
    w1 = jnp.concatenate([w_in[..., :cuts[6]], qc_pad, kc, vc2, fbp], axis=-1)
    assert w1.shape[-1] == _W1_COLS
    return w1.astype(bf16), w_in[..., cuts[10]:].astype(bf16)


def _bias_lookup(tbl, dist):
    bucket = _t5_bucket(jnp.asarray(dist, jnp.int32))
    out = jnp.zeros((tbl.shape[0],) + bucket.shape, f32)
    for b in range(NUM_BUCKETS):
        col = tbl[:, b].astype(f32).reshape((-1,) + (1,) * bucket.ndim)
        out = jnp.where(bucket[None] == b, col, out)
    return out


def _toeplitz_kernel(u_ref, o_ref):
    x = jnp.broadcast_to(u_ref[0, 0], (TA, 2 * TA))
    o_ref[0, 0] = pltpu.roll(x, 0, 1, stride=1, stride_axis=0)[:, :TA]


def _moba_bias_tiles(tbl_a, s):
    near = min(-(-(MAX_DISTANCE + TA - 1) // TA), s // TA)
    m = np.arange(2 * TA)
    rel = np.where(m < TA, -m, np.where(m == TA, 0, 2 * TA - m))
    dist = np.arange(near)[:, None] * TA + rel[None, :]
    gen = jnp.where(dist >= 0, _bias_lookup(tbl_a, np.maximum(dist, 0)) * LOG2E, NEG)
    tiles = pl.pallas_call(
        _toeplitz_kernel,
        grid=(N_HEADS, near),
        in_specs=[pl.BlockSpec((1, 1, 1, 2 * TA), lambda h, d: (h, d, 0, 0))],
        out_specs=pl.BlockSpec((1, 1, TA, TA), lambda h, d: (h, d, 0, 0)),
        out_shape=jax.ShapeDtypeStruct((N_HEADS, near, TA, TA), f32),
        compiler_params=_params("arbitrary", "arbitrary"),
        name="moba_bias_tiles",
    )(gen[:, :, None, :])
    far = _bias_lookup(tbl_a, np.asarray([near * TA]))[:, 0] * LOG2E
    return tiles, far


def _swa_tables(tbl_c, sinks):
    tl = np.arange(Q_BLOCK)
    sl = np.arange(2 * Q_BLOCK)
    dist = tl[:, None] + Q_BLOCK - sl[None, :]
    bias = _bias_lookup(tbl_c, dist)
    band = (dist >= 0) & (dist < WINDOW)
    assert not band[:, 0].any()
    masks = np.stack([band, band & (sl[None, :] >= Q_BLOCK)])
    table = jnp.where(masks[:, None], bias[None], NEG)
    sink = sinks.astype(f32)[:, None, :, None, None]
    table = jnp.where((sl == 0)[None, None, None, None, :], sink, table[None])
    return table.reshape(sinks.shape[0], 2, C_HEADS // C_GROUP, C_GROUP * Q_BLOCK, 2 * Q_BLOCK)


def kernel(x, ln1, w_in, b_f, sinks, w_pa, w_pb, w_pc, w_o, ln2, w_up, conv_w, conv_b, w_down,
           rel_bias, ln_f):
    bsz, s, _ = x.shape
    assert bsz == 1 and s % TM == 0 and s % TA == 0 and TM % MOBA_BLOCK == 0
    assert s // MOBA_BLOCK <= HEAD_DIM
    w1, wg = _fused_in_weight(w_in)
    wpa, wpb, wpc, wo, wup, wdn = (w.astype(bf16) for w in (w_pa, w_pb, w_pc, w_o, w_up, w_down))
    tbl_a = rel_bias[:, :N_HEADS].T
    tbl_c = rel_bias[:, N_HEADS:].T
    moba_tiles, moba_far = _moba_bias_tiles(tbl_a, s)
    swa_bias = _swa_tables(tbl_c, sinks)
    bf_lane = jnp.pad(b_f.astype(f32), ((0, 0), (0, LANES - N_HEADS)))[:, None, :]
    gf = ln_f.astype(f32)[None, :]

    x2 = x[0]
    for l in range(DEPTH):
        qa, ka, va, qb, kb, vb, c = _inproj(x2, ln1[l][None, :], w1, bf_lane[l], l)
        oa = _moba_flash(moba_far, qa, ka, va, moba_tiles)
        ob = _fox_flash(qb, kb, vb)
        oc = _swa(c, swa_bias, l)
        x2 = _mix_ffn(x2, ln1[l][None, :], wg, oa, ob, oc, wpa, wpb, wpc, wo,
                      ln2[l][None, :], wup, conv_w[l], conv_b[l][None, :], wdn, gf, l)
    return x2[None]
```

```python
import functools
import math

import numpy as np
import jax
import jax.numpy as jnp
from jax import lax
from jax.experimental import pallas as pl
from jax.experimental.pallas import tpu as pltpu

D_MODEL = 1024
DEPTH = 4
HEAD_DIM = 64
N_HEADS = 4
C_HEADS = 8
C_GROUP = 4
MOBA_BLOCK = 256
MOBA_TOPK = 3
Q_BLOCK = 128
WINDOW = 128
NUM_BUCKETS = 32
MAX_DISTANCE = 4096
D_FF = 2816
EPS = 1e-6
NEG = -1e30
SCALE = HEAD_DIM ** -0.5
LOG2E = math.log2(math.e)

LANES = 128
HALO = 8
TM = 512
TA = 1024
FLASH_UNROLL = 4
FF_CHUNK = 256
VMEM_LIMIT = 56 * 1024 * 1024

_QKV_A = (0, 768)
_QKV_B = (768, 1536)
_C_ALL = (1536, 2944)
_F_B = (2944, 3072)
_W1_COLS = 3072
_C_K = 1024
_C_V = 1152
_C_W = 1408

bf16 = jnp.bfloat16
f32 = jnp.float32
_HI = lax.Precision.HIGHEST


def _rms(x, g):
    ms = jnp.mean(x * x, axis=-1, keepdims=True)
    return x * lax.rsqrt(ms + EPS) * g


def _t5_bucket(dist):
    dist = jnp.maximum(dist, 0)
    max_exact = NUM_BUCKETS // 2
    d = jnp.maximum(dist.astype(jnp.float32), 1.0)
    large = max_exact + (jnp.log(d / max_exact) / math.log(MAX_DISTANCE / max_exact)
                         * (NUM_BUCKETS - max_exact)).astype(jnp.int32)
    large = jnp.minimum(large, NUM_BUCKETS - 1)
    return jnp.where(dist < max_exact, dist, large)


def _params(*sem):
    return pltpu.CompilerParams(dimension_semantics=sem, vmem_limit_bytes=VMEM_LIMIT)


def _resident(shape, index_map):
    return pl.BlockSpec(shape, index_map, pipeline_mode=pl.Buffered(1))


def _layer_weight(w, layer):
    return _resident((None,) + w.shape[1:], lambda *_: (layer,) + (0,) * (w.ndim - 1))


def _head_lanes(z, h, part):
    lo = N_HEADS * HEAD_DIM * part + LANES * (h // 2)
    col = z[:, lo:lo + LANES]
    return pltpu.roll(col, HEAD_DIM, 1) if h % 2 else col


def _moba_operands(i, za, km_ref, q_ref, k_ref, v_ref, nblk):
    width = N_HEADS * HEAD_DIM
    lane = lax.broadcasted_iota(jnp.int32, (TM, LANES), 1)
    kblk = (i * TM + lax.broadcasted_iota(jnp.int32, (TM, LANES), 0)) >> 8
    bidx = lax.broadcasted_iota(jnp.int32, (nblk, TM), 0)
    bidx_f = bidx.astype(f32)
    qblk = (i * TM + lax.broadcasted_iota(jnp.int32, (nblk, TM), 1)) >> 8
    member = jnp.where(bidx == qblk, 1.0 / MOBA_BLOCK, 0.0).astype(f32)
    km_ref[...] += jnp.dot(member, za[:, width:2 * width], precision=_HI, preferred_element_type=f32)
    km = km_ref[...]
    km_head = lax.broadcasted_iota(jnp.int32, (nblk, width), 1) >> 6
    q_all = za[:, 0:width]
    for h in range(N_HEADS):
        gate = lax.dot_general(jnp.where(km_head == h, km, 0.0), q_all, (((1,), (1,)), ((), ())),
                               precision=_HI, preferred_element_type=f32)
        gate = jnp.where(bidx < qblk, gate, -jnp.inf)
        sel = jnp.zeros((nblk, TM), f32)
        for _ in range(MOBA_TOPK):
            m = jnp.max(gate, axis=0, keepdims=True)
            first = jnp.min(jnp.where(gate == m, bidx_f, float(nblk)), axis=0, keepdims=True)
            hit = jnp.logical_and(bidx_f == first, m > -jnp.inf)
            sel = jnp.where(hit, 1.0, sel)
            gate = jnp.where(hit, -jnp.inf, gate)
        notsel = jnp.where(bidx == qblk, 0.0, 1.0 - sel)
        pad = jnp.zeros((LANES - nblk, TM), f32)
        if nblk == HEAD_DIM:
            ns = jnp.concatenate([pad, notsel], axis=0).T
        else:
            ns = jnp.concatenate([pad[:HEAD_DIM], notsel, pad[HEAD_DIM:]], axis=0).T
        q_ref[h] = jnp.where(lane < HEAD_DIM, _head_lanes(za, h, 0) * (SCALE * LOG2E), ns).astype(bf16)
        k_ref[h] = jnp.where(lane < HEAD_DIM, _head_lanes(za, h, 1),
                             jnp.where(lane == HEAD_DIM + kblk, NEG, 0.0)).astype(bf16)
        v_ref[h] = jnp.where(lane < HEAD_DIM, _head_lanes(za, h, 2),
                             jnp.where(lane == HEAD_DIM, 1.0, 0.0)).astype(bf16)


def _fox_operands(zb, fb, cum_ref, q_ref, k_ref, v_ref):
    logf = jnp.minimum(fb, 0.0) - jnp.log1p(jnp.exp(-jnp.abs(fb)))
    r = lax.broadcasted_iota(jnp.int32, (TM, TM), 0)
    c = lax.broadcasted_iota(jnp.int32, (TM, TM), 1)
    tri = jnp.where(c <= r, 1.0, 0.0).astype(f32)
    cum = jnp.dot(tri, logf, precision=_HI, preferred_element_type=f32) + cum_ref[...]
    cum_ref[...] = cum[TM - 1:TM, :]
    lane = lax.broadcasted_iota(jnp.int32, (TM, LANES), 1)
    d = HEAD_DIM
    for h in range(N_HEADS):
        cb = jnp.broadcast_to(cum[:, h:h + 1], (TM, LANES)) * LOG2E
        hi = cb.astype(bf16).astype(f32)
        r1 = cb - hi
        mid = r1.astype(bf16).astype(f32)
        lo = r1 - mid
        qa = jnp.where(lane < d, _head_lanes(zb, h, 0) * (SCALE * LOG2E),
             jnp.where(lane == d, hi,
             jnp.where(lane == d + 1, mid,
             jnp.where(lane == d + 2, lo,
             jnp.where(lane < d + 6, 1.0, 0.0)))))
        q_ref[h] = qa.astype(bf16)
        ka = jnp.where(lane < d, _head_lanes(zb, h, 1),
             jnp.where(lane < d + 3, 1.0,
             jnp.where(lane == d + 3, -hi,
             jnp.where(lane == d + 4, -mid,
             jnp.where(lane == d + 5, -lo, 0.0)))))
        k_ref[h] = ka.astype(bf16)
        v_ref[h] = jnp.where(lane < d, _head_lanes(zb, h, 2),
                             jnp.where(lane == d, 1.0, 0.0)).astype(bf16)


def _inproj_kernel(x_ref, g_ref, w_ref, bf_ref, qa_ref, ka_ref, va_ref, qb_ref, kb_ref, vb_ref, c_ref,
                   km_ref, cum_ref, za_ref, zb_ref, fb_ref, *, nblk):
    i = pl.program_id(0)

    @pl.when(i == 0)
    def _():
        km_ref[...] = jnp.zeros_like(km_ref)
        cum_ref[...] = jnp.zeros_like(cum_ref)

    h = _rms(x_ref[...], g_ref[...]).astype(bf16)

    def mm(span):
        return jnp.dot(h, w_ref[:, span[0]:span[1]], preferred_element_type=f32)

    za_ref[...] = mm(_QKV_A)
    zb_ref[...] = mm(_QKV_B)
    fb_ref[...] = mm(_F_B) + bf_ref[...]
    c_ref[...] = mm(_C_ALL).astype(bf16)
    _moba_operands(i, za_ref, km_ref, qa_ref, ka_ref, va_ref, nblk)
    _fox_operands(zb_ref, fb_ref[...], cum_ref, qb_ref, kb_ref, vb_ref)


def _inproj(x2, g, w1, bf_lane, layer):
    s = x2.shape[0]
    nblk = s // MOBA_BLOCK
    hb = pl.BlockSpec((N_HEADS, TM, LANES), lambda i: (0, i, 0))
    aug = jax.ShapeDtypeStruct((N_HEADS, s, LANES), bf16)
    return pl.pallas_call(
        functools.partial(_inproj_kernel, nblk=nblk),
        grid=(s // TM,),
        in_specs=[pl.BlockSpec((TM, D_MODEL), lambda i: (i, 0)),
                  pl.BlockSpec((1, D_MODEL), lambda i: (0, 0)),
                  _layer_weight(w1, layer),
                  pl.BlockSpec((1, LANES), lambda i: (0, 0))],
        out_specs=[hb] * 6 + [pl.BlockSpec((TM, _C_W), lambda i: (i, 0))],
        out_shape=[aug] * 6 + [jax.ShapeDtypeStruct((s, _C_W), bf16)],
        scratch_shapes=[pltpu.VMEM((nblk, N_HEADS * HEAD_DIM), f32), pltpu.VMEM((1, LANES), f32),
                        pltpu.VMEM((TM, _QKV_A[1] - _QKV_A[0]), f32),
                        pltpu.VMEM((TM, _QKV_B[1] - _QKV_B[0]), f32), pltpu.VMEM((TM, LANES), f32)],
        compiler_params=_params("arbitrary"),
        name="inproj",
    )(x2, g, w1, bf_lane)


def _flash_step(q, k_ref, v_ref, j, carry, add_bias):
    m, acc = carry
    off = pl.multiple_of(j * TA, TA)
    k = k_ref[0, pl.ds(off, TA), :]
    v = v_ref[0, pl.ds(off, TA), :]
    s = lax.dot_general(q, k, (((1,), (1,)), ((), ())), preferred_element_type=f32)
    s = add_bias(s)
    m_new = jnp.maximum(m, jnp.max(s, axis=-1, keepdims=True))
    alpha = jnp.exp2(m - m_new)
    p = jnp.exp2(s - m_new)
    acc = alpha * acc + jnp.dot(p.astype(bf16), v, preferred_element_type=f32)
    return m_new, acc


def _flash_loop(lo, hi, step, carry):
    for width in (FLASH_UNROLL, 2, 1):
        trips = (hi - lo) // width

        def body(t, c, width=width, lo=lo):
            for u in range(width):
                c = step(lo + width * t + u, c)
            return c

        carry = lax.fori_loop(0, trips, body, carry)
        lo = lo + width * trips
    return carry


def _flash_finish(o_ref, acc):
    o_ref[0] = (acc / acc[:, HEAD_DIM:HEAD_DIM + 1]).astype(o_ref.dtype)


def _flash_init():
    return jnp.full((TA, 1), -jnp.inf, f32), jnp.zeros((TA, LANES), f32)


def _fox_flash_kernel(q_ref, k_ref, v_ref, o_ref):
    i = pl.program_id(1)
    q = q_ref[0]
    carry = _flash_loop(
        0, i, lambda j, c: _flash_step(q, k_ref, v_ref, j, c, lambda s: s), _flash_init())
    row = lax.broadcasted_iota(jnp.int32, (TA, TA), 0)
    col = lax.broadcasted_iota(jnp.int32, (TA, TA), 1)
    _, acc = _flash_step(q, k_ref, v_ref, i, carry, lambda s: jnp.where(col <= row, s, NEG))
    _flash_finish(o_ref, acc)


def _moba_flash_kernel(far_ref, q_ref, k_ref, v_ref, bias_ref, o_ref, *, near):
    h = pl.program_id(0)
    i = pl.program_id(1)
    q = q_ref[0]
    far = far_ref[h]
    n_far = jnp.maximum(i - (near - 1), 0)
    carry = _flash_loop(
        0, n_far, lambda j, c: _flash_step(q, k_ref, v_ref, j, c, lambda s: s + far), _flash_init())
    carry = _flash_loop(
        n_far, i + 1,
        lambda j, c: _flash_step(q, k_ref, v_ref, j, c, lambda s: s + bias_ref[0, i - j]), carry)
    _flash_finish(o_ref, carry[1])


def _flash_specs(s):
    qspec = pl.BlockSpec((1, TA, LANES), lambda h, i: (h, i, 0))
    kvspec = _resident((1, s, LANES), lambda h, i: (h, 0, 0))
    return qspec, kvspec


def _fox_flash(q, k, v):
    s = q.shape[1]
    qspec, kvspec = _flash_specs(s)
    return pl.pallas_call(
        _fox_flash_kernel,
        grid=(N_HEADS, s // TA),
        in_specs=[qspec, kvspec, kvspec],
        out_specs=qspec,
        out_shape=jax.ShapeDtypeStruct((N_HEADS, s, LANES), bf16),
        compiler_params=_params("arbitrary", "arbitrary"),
        name="fox_flash",
    )(q, k, v)


def _moba_flash(far, q, k, v, bias):
    s = q.shape[1]
    near = bias.shape[1]
    qspec, kvspec = _flash_specs(s)
    return pl.pallas_call(
        functools.partial(_moba_flash_kernel, near=near),
        grid=(N_HEADS, s // TA),
        in_specs=[pl.BlockSpec(memory_space=pltpu.SMEM), qspec, kvspec, kvspec,
                  _resident((1, near, TA, TA), lambda h, i: (h, 0, 0, 0))],
        out_specs=qspec,
        out_shape=jax.ShapeDtypeStruct((N_HEADS, s, LANES), bf16),
        compiler_params=_params("arbitrary", "arbitrary"),
        name="moba_flash",
    )(far, q, k, v, bias)


def _swa_kernel(cur_ref, prev_ref, bias_ref, o_ref):
    i = pl.program_id(0)
    first = jnp.where(i == 0, 1, 0)
    lane = lax.broadcasted_iota(jnp.int32, (Q_BLOCK, LANES), 1)
    qb = Q_BLOCK
    ones = jnp.ones((2 * qb, LANES), bf16)
    for b in range(TM // qb):
        if b == 0:
            kband = jnp.concatenate([prev_ref[TM - qb:TM, _C_K:_C_V], cur_ref[0:qb, _C_K:_C_V]], axis=0)
            vband = jnp.concatenate([prev_ref[TM - qb:TM, _C_V:_C_W], cur_ref[0:qb, _C_V:_C_W]], axis=0)
            variant = first
        else:
            kband = cur_ref[qb * (b - 1):qb * (b + 1), _C_K:_C_V]
            vband = cur_ref[qb * (b - 1):qb * (b + 1), _C_V:_C_W]
            variant = 0
        kband = jnp.where(lax.broadcasted_iota(jnp.int32, kband.shape, 0) == 0, 0.0, kband).astype(bf16)
        vband = jnp.where(lax.broadcasted_iota(jnp.int32, vband.shape, 0) == 0, 0.0, vband).astype(bf16)
        outs = []
        for g in range(C_HEADS // C_GROUP):
            qs = jnp.concatenate(
                [cur_ref[qb * b:qb * (b + 1), LANES * (C_GROUP * g + u):LANES * (C_GROUP * g + u + 1)]
                 for u in range(C_GROUP)], axis=0)
            s = lax.dot_general(qs, kband, (((1,), (1,)), ((), ())), preferred_element_type=f32)
            s = s + bias_ref[variant, g]
            p = jnp.exp(s - jnp.max(s, axis=-1, keepdims=True)).astype(bf16)
            den = jnp.dot(p, ones, preferred_element_type=f32)
            for u in range(C_GROUP):
                half = (C_GROUP * g + u) % 2
                vg = vband[:, 0:LANES] if g == half else vband[:, LANES:2 * LANES]
                pv = jnp.dot(p[qb * u:qb * (u + 1)], vg, preferred_element_type=f32)
                outs.append(pv / den[qb * u:qb * (u + 1)])
        for pr in range(C_HEADS // 2):
            o_ref[qb * b:qb * (b + 1), LANES * pr:LANES * (pr + 1)] = jnp.where(
                lane < HEAD_DIM, outs[2 * pr], outs[2 * pr + 1]).astype(o_ref.dtype)


def _swa(c, bias, layer):
    s = c.shape[0]
    return pl.pallas_call(
        _swa_kernel,
        grid=(s // TM,),
        in_specs=[pl.BlockSpec((TM, _C_W), lambda i: (i, 0)),
                  pl.BlockSpec((TM, _C_W), lambda i: (jnp.maximum(i - 1, 0), 0)),
                  pl.BlockSpec((None,) + bias.shape[1:], lambda i: (layer, 0, 0, 0, 0))],
        out_specs=pl.BlockSpec((TM, 512), lambda i: (i, 0)),
        out_shape=jax.ShapeDtypeStruct((s, 512), bf16),
        compiler_params=_params("arbitrary"),
        name="swa",
    )(c, c, bias)


def _mix_ffn_kernel(x_ref, g1_ref, wg_ref, oa_ref, ob_ref, oc_ref, wpa_ref, wpb_ref, wpc_ref, wo_ref,
                    g2_ref, wup_ref, cw_ref, cb_ref, wdn_ref, gf_ref, out_ref,
                    mg_ref, halo_ref, u_ref, act_ref, *, final):
    @pl.when(pl.program_id(0) == 0)
    def _():
        halo_ref[...] = jnp.zeros_like(halo_ref)

    x = x_ref[...]
    h = _rms(x, g1_ref[...]).astype(bf16)
    oc = oc_ref[...]
    lane = lax.broadcasted_iota(jnp.int32, (TM, LANES), 1)

    def head_pairs(o_ref):
        return [jnp.where(lane < HEAD_DIM, o_ref[2 * a].astype(f32),
                          pltpu.roll(o_ref[2 * a + 1].astype(f32), HEAD_DIM, 1)).astype(bf16)
                for a in range(N_HEADS // 2)]

    oa, ob = head_pairs(oa_ref), head_pairs(ob_ref)
    cw = 512
    for n in range(D_MODEL // cw):
        lo = cw * n

        def heads(pairs, w_ref):
            acc = jnp.dot(pairs[0], w_ref[0:LANES, lo:lo + cw], preferred_element_type=f32)
            for a in range(1, N_HEADS // 2):
                acc += jnp.dot(pairs[a], w_ref[LANES * a:LANES * (a + 1), lo:lo + cw],
                               preferred_element_type=f32)
            return acc

        def gate(which):
            z = jnp.dot(h, wg_ref[:, D_MODEL * which + lo:D_MODEL * which + lo + cw],
                        preferred_element_type=f32)
            return jax.nn.sigmoid(z)

        mg = gate(0) * heads(oa, wpa_ref)
        mg += gate(1) * heads(ob, wpb_ref)
        mg += gate(2) * jnp.dot(oc, wpc_ref[:, lo:lo + cw], preferred_element_type=f32)
        mg_ref[:, lo:lo + cw] = mg.astype(bf16)
    x = x + jnp.dot(mg_ref[...], wo_ref[...], preferred_element_type=f32)

    h2 = _rms(jnp.concatenate([halo_ref[...], x], axis=0), g2_ref[...]).astype(bf16)
    halo_ref[...] = x[TM - HALO:TM]

    def conv(lo):
        u_ref[...] = jnp.dot(h2, wup_ref[:, lo:lo + FF_CHUNK], preferred_element_type=f32)
        w = cw_ref[:, lo:lo + FF_CHUNK]
        return (w[2:3] * u_ref[HALO:HALO + TM] + w[1:2] * u_ref[HALO - 1:HALO - 1 + TM]
                + w[0:1] * u_ref[HALO - 2:HALO - 2 + TM] + cb_ref[:, lo:lo + FF_CHUNK])

    for c in range(D_FF // FF_CHUNK):
        a = conv(FF_CHUNK * c)
        b = conv(D_FF + FF_CHUNK * c)
        act_ref[:, FF_CHUNK * c:FF_CHUNK * (c + 1)] = (a * jax.nn.sigmoid(a) * b).astype(bf16)
    o = x + jnp.dot(act_ref[...], wdn_ref[...], preferred_element_type=f32)
    if final:
        o = _rms(o, gf_ref[...])
    out_ref[...] = o


def _mix_ffn(x2, g1, wg, oa, ob, oc, wpa, wpb, wpc, wo, g2, wup, cw, cb, wdn, gf, layer):
    s = x2.shape[0]
    hb = pl.BlockSpec((N_HEADS, TM, LANES), lambda i: (0, i, 0))
    full = lambda a: _layer_weight(a, layer)
    vec = lambda a: pl.BlockSpec(a.shape, lambda i: (0,) * a.ndim)
    row = lambda w: pl.BlockSpec((TM, w), lambda i: (i, 0))
    return pl.pallas_call(
        functools.partial(_mix_ffn_kernel, final=(layer == DEPTH - 1)),
        grid=(s // TM,),
        in_specs=[row(D_MODEL), vec(g1), full(wg), hb, hb, row(512),
                  full(wpa), full(wpb), full(wpc), full(wo),
                  vec(g2), full(wup), vec(cw), vec(cb), full(wdn), vec(gf)],
        out_specs=row(D_MODEL),
        out_shape=jax.ShapeDtypeStruct((s, D_MODEL), f32),
        scratch_shapes=[pltpu.VMEM((TM, D_MODEL), bf16), pltpu.VMEM((HALO, D_MODEL), f32),
                        pltpu.VMEM((TM + HALO, FF_CHUNK), f32), pltpu.VMEM((TM, D_FF), bf16)],
        compiler_params=_params("arbitrary"),
        name="mix_ffn",
    )(x2, g1, wg, oa, ob, oc, wpa, wpb, wpc, wo, g2, wup, cw, cb, wdn, gf)


def _fused_in_weight(w_in):
    l = w_in.shape[0]
    wb = w_in.astype(bf16)
    cuts = np.cumsum([0, 256, 256, 256, 256, 256, 256, N_HEADS, 512, 128, 128]).tolist()
    fb, qc, kc, vc = (wb[..., cuts[n]:cuts[n + 1]] for n in range(6, 10))
    qc = (qc * SCALE).reshape(l, D_MODEL, C_HEADS, HEAD_DIM)
    zero = jnp.zeros_like(qc[:, :, 0])
    qc_pad = jnp.concatenate(
        [jnp.concatenate([qc[:, :, hq], zero] if hq < C_GROUP else [zero, qc[:, :, hq]], axis=-1)
         for hq in range(C_HEADS)], axis=-1)
    vc2 = jnp.concatenate([vc, vc[..., HEAD_DIM:], vc[..., :HEAD_DIM]], axis=-1)
    fbp = jnp.pad(fb, ((0, 0), (0, 0), (0, LANES - N_HEADS)))
    w1 = jnp.concatenate([wb[..., :cuts[6]], qc_pad, kc, vc2, fbp], axis=-1)
    assert w1.shape[-1] == _W1_COLS and w1.dtype == bf16
    return w1, wb[..., cuts[10]:]


def _bias_lookup(tbl, dist):
    bucket = _t5_bucket(jnp.asarray(dist, jnp.int32))
    out = jnp.zeros((tbl.shape[0],) + bucket.shape, f32)
    for b in range(NUM_BUCKETS):
        col = tbl[:, b].astype(f32).reshape((-1,) + (1,) * bucket.ndim)
        out = jnp.where(bucket[None] == b, col, out)
    return out


def _toeplitz_kernel(u_ref, o_ref):
    x = jnp.broadcast_to(u_ref[0, 0], (TA, 2 * TA))
    o_ref[0, 0] = pltpu.roll(x, 0, 1, stride=1, stride_axis=0)[:, :TA]


def _moba_bias_tiles(tbl_a, s):
    near = min(-(-(MAX_DISTANCE + TA - 1) // TA), s // TA)
    m = np.arange(2 * TA)
    rel = np.where(m < TA, -m, np.where(m == TA, 0, 2 * TA - m))
    dist = np.arange(near)[:, None] * TA + rel[None, :]
    gen = jnp.where(dist >= 0, _bias_lookup(tbl_a, np.maximum(dist, 0)) * LOG2E, NEG)
    tiles = pl.pallas_call(
        _toeplitz_kernel,
        grid=(N_HEADS, near),
        in_specs=[pl.BlockSpec((1, 1, 1, 2 * TA), lambda h, d: (h, d, 0, 0))],
        out_specs=pl.BlockSpec((1, 1, TA, TA), lambda h, d: (h, d, 0, 0)),
        out_shape=jax.ShapeDtypeStruct((N_HEADS, near, TA, TA), f32),
        compiler_params=_params("arbitrary", "arbitrary"),
        name="moba_bias_tiles",
    )(gen[:, :, None, :])
    far = _bias_lookup(tbl_a, np.asarray([near * TA]))[:, 0] * LOG2E
    return tiles, far


def _swa_tables(tbl_c, sinks):
    tl = np.arange(Q_BLOCK)
    sl = np.arange(2 * Q_BLOCK)
    dist = tl[:, None] + Q_BLOCK - sl[None, :]
    bias = _bias_lookup(tbl_c, dist)
    band = (dist >= 0) & (dist < WINDOW)
    assert not band[:, 0].any()
    masks = np.stack([band, band & (sl[None, :] >= Q_BLOCK)])
    table = jnp.where(masks[:, None], bias[None], NEG)
    sink = sinks.astype(f32)[:, None, :, None, None]
    table = jnp.where((sl == 0)[None, None, None, None, :], sink, table[None])
    return table.reshape(sinks.shape[0], 2, C_HEADS // C_GROUP, C_GROUP * Q_BLOCK, 2 * Q_BLOCK)


def kernel(x, ln1, w_in, b_f, sinks, w_pa, w_pb, w_pc, w_o, ln2, w_up, conv_w, conv_b, w_down,
           rel_bias, ln_f):
    bsz, s, _ = x.shape
    assert bsz == 1 and s % TM == 0 and s % TA == 0 and TM % MOBA_BLOCK == 0
    assert s // MOBA_BLOCK <= HEAD_DIM
    w1, wg = _fused_in_weight(w_in)
    wpa, wpb, wpc, wo, wup, wdn = (w.astype(bf16) for w in (w_pa, w_pb, w_pc, w_o, w_up, w_down))
    tbl_a = rel_bias[:, :N_HEADS].T
    tbl_c = rel_bias[:, N_HEADS:].T
    moba_tiles, moba_far = _moba_bias_tiles(tbl_a, s)
    swa_bias = _swa_tables(tbl_c, sinks)
    bf_lane = jnp.pad(b_f.astype(f32), ((0, 0), (0, LANES - N_HEADS)))[:, None, :]
    gf = ln_f.astype(f32)[None, :]

    x2 = x[0]
    for l in range(DEPTH):
        qa, ka, va, qb, kb, vb, c = _inproj(x2, ln1[l][None, :], w1, bf_lane[l], l)
        oa = _moba_flash(moba_far, qa, ka, va, moba_tiles)
        ob = _fox_flash(qb, kb, vb)
        oc = _swa(c, swa_bias, l)
        x2 = _mix_ffn(x2, ln1[l][None, :], wg, oa, ob, oc, wpa, wpb, wpc, wo,
                      ln2[l][None, :], wup, conv_w[l], conv_b[l][None, :], wdn, gf, l)
    return x2[None]
```

```python
import functools
import math

import numpy as np
import jax
import jax.numpy as jnp
from jax import lax
from jax.experimental import pallas as pl
from jax.experimental.pallas import tpu as pltpu

D_MODEL = 1024
DEPTH = 4
HEAD_DIM = 64
N_HEADS = 4
C_HEADS = 8
C_GROUP = 4
MOBA_BLOCK = 256
MOBA_TOPK = 3
Q_BLOCK = 128
WINDOW = 128
NUM_BUCKETS = 32
MAX_DISTANCE = 4096
D_FF = 2816
EPS = 1e-6
NEG = -1e30
SCALE = HEAD_DIM ** -0.5
LOG2E = math.log2(math.e)

LANES = 128
HALO = 8
TM = 512
TA = 1024
FLASH_UNROLL = 4
FF_CHUNK = 256
VMEM_LIMIT = 56 * 1024 * 1024

_QKV_A = (0, 768)
_QKV_B = (768, 1536)
_C_ALL = (1536, 2944)
_F_B = (2944, 3072)
_W1_COLS = 3072
_C_K = 1024
_C_V = 1152
_C_W = 1408

bf16 = jnp.bfloat16
f32 = jnp.float32
_HI = lax.Precision.HIGHEST


def _rms(x, g):
    ms = jnp.mean(x * x, axis=-1, keepdims=True)
    return x * lax.rsqrt(ms + EPS) * g


def _t5_bucket(dist):
    dist = jnp.maximum(dist, 0)
    max_exact = NUM_BUCKETS // 2
    d = jnp.maximum(dist.astype(jnp.float32), 1.0)
    large = max_exact + (jnp.log(d / max_exact) / math.log(MAX_DISTANCE / max_exact)
                         * (NUM_BUCKETS - max_exact)).astype(jnp.int32)
    large = jnp.minimum(large, NUM_BUCKETS - 1)
    return jnp.where(dist < max_exact, dist, large)


def _params(*sem):
    return pltpu.CompilerParams(dimension_semantics=sem, vmem_limit_bytes=VMEM_LIMIT)


def _resident(shape, index_map):
    return pl.BlockSpec(shape, index_map, pipeline_mode=pl.Buffered(1))


def _layer_weight(w, layer):
    return _resident((None,) + w.shape[1:], lambda *_: (layer,) + (0,) * (w.ndim - 1))


def _head_lanes(z, h, part):
    lo = N_HEADS * HEAD_DIM * part + LANES * (h // 2)
    col = z[:, lo:lo + LANES]
    return pltpu.roll(col, HEAD_DIM, 1) if h % 2 else col


def _moba_operands(i, za, km_ref, q_ref, k_ref, v_ref, nblk):
    width = N_HEADS * HEAD_DIM
    lane = lax.broadcasted_iota(jnp.int32, (TM, LANES), 1)
    kblk = (i * TM + lax.broadcasted_iota(jnp.int32, (TM, LANES), 0)) >> 8
    bidx = lax.broadcasted_iota(jnp.int32, (nblk, TM), 0)
    bidx_f = bidx.astype(f32)
    qblk = (i * TM + lax.broadcasted_iota(jnp.int32, (nblk, TM), 1)) >> 8
    member = jnp.where(bidx == qblk, 1.0 / MOBA_BLOCK, 0.0).astype(f32)
    km_ref[...] += jnp.dot(member, za[:, width:2 * width], precision=_HI, preferred_element_type=f32)
    km = km_ref[...]
    km_head = lax.broadcasted_iota(jnp.int32, (nblk, width), 1) >> 6
    q_all = za[:, 0:width]
    for h in range(N_HEADS):
        gate = lax.dot_general(jnp.where(km_head == h, km, 0.0), q_all, (((1,), (1,)), ((), ())),
                               precision=_HI, preferred_element_type=f32)
        gate = jnp.where(bidx < qblk, gate, -jnp.inf)
        sel = jnp.zeros((nblk, TM), f32)
        for _ in range(MOBA_TOPK):
            m = jnp.max(gate, axis=0, keepdims=True)
            first = jnp.min(jnp.where(gate == m, bidx_f, float(nblk)), axis=0, keepdims=True)
            hit = jnp.logical_and(bidx_f == first, m > -jnp.inf)
            sel = jnp.where(hit, 1.0, sel)
            gate = jnp.where(hit, -jnp.inf, gate)
        notsel = jnp.where(bidx == qblk, 0.0, 1.0 - sel)
        pad = jnp.zeros((LANES - nblk, TM), f32)
        if nblk == HEAD_DIM:
            ns = jnp.concatenate([pad, notsel], axis=0).T
        else:
            ns = jnp.concatenate([pad[:HEAD_DIM], notsel, pad[HEAD_DIM:]], axis=0).T
        q_ref[h] = jnp.where(lane < HEAD_DIM, _head_lanes(za, h, 0) * (SCALE * LOG2E), ns).astype(bf16)
        k_ref[h] = jnp.where(lane < HEAD_DIM, _head_lanes(za, h, 1),
                             jnp.where(lane == HEAD_DIM + kblk, NEG, 0.0)).astype(bf16)
        v_ref[h] = jnp.where(lane < HEAD_DIM, _head_lanes(za, h, 2),
                             jnp.where(lane == HEAD_DIM, 1.0, 0.0)).astype(bf16)


def _fox_operands(zb, fb, cum_ref, q_ref, k_ref, v_ref):
    logf = jnp.minimum(fb, 0.0) - jnp.log1p(jnp.exp(-jnp.abs(fb)))
    r = lax.broadcasted_iota(jnp.int32, (TM, TM), 0)
    c = lax.broadcasted_iota(jnp.int32, (TM, TM), 1)
    tri = jnp.where(c <= r, 1.0, 0.0).astype(f32)
    cum = jnp.dot(tri, logf, precision=_HI, preferred_element_type=f32) + cum_ref[...]
    cum_ref[...] = cum[TM - 1:TM, :]
    lane = lax.broadcasted_iota(jnp.int32, (TM, LANES), 1)
    d = HEAD_DIM
    for h in range(N_HEADS):
        cb = jnp.broadcast_to(cum[:, h:h + 1], (TM, LANES)) * LOG2E
        hi = cb.astype(bf16).astype(f32)
        r1 = cb - hi
        mid = r1.astype(bf16).astype(f32)
        lo = r1 - mid
        qa = jnp.where(lane < d, _head_lanes(zb, h, 0) * (SCALE * LOG2E),
             jnp.where(lane == d, hi,
             jnp.where(lane == d + 1, mid,
             jnp.where(lane == d + 2, lo,
             jnp.where(lane < d + 6, 1.0, 0.0)))))
        q_ref[h] = qa.astype(bf16)
        ka = jnp.where(lane < d, _head_lanes(zb, h, 1),
             jnp.where(lane < d + 3, 1.0,
             jnp.where(lane == d + 3, -hi,
             jnp.where(lane == d + 4, -mid,
             jnp.where(lane == d + 5, -lo, 0.0)))))
        k_ref[h] = ka.astype(bf16)
        v_ref[h] = jnp.where(lane < d, _head_lanes(zb, h, 2),
                             jnp.where(lane == d, 1.0, 0.0)).astype(bf16)


def _inproj_kernel(x_ref, g_ref, w_ref, bf_ref, qa_ref, ka_ref, va_ref, qb_ref, kb_ref, vb_ref, c_ref,
                   km_ref, cum_ref, za_ref, zb_ref, fb_ref, *, nblk):
    i = pl.program_id(0)

    @pl.when(i == 0)
    def _():
        km_ref[...] = jnp.zeros_like(km_ref)
        cum_ref[...] = jnp.zeros_like(cum_ref)

    h = _rms(x_ref[...], g_ref[...]).astype(bf16)

    def mm(span):
        return jnp.dot(h, w_ref[:, span[0]:span[1]], preferred_element_type=f32)

    za_ref[...] = mm(_QKV_A)
    zb_ref[...] = mm(_QKV_B)
    fb_ref[...] = mm(_F_B) + bf_ref[...]
    c_ref[...] = mm(_C_ALL).astype(bf16)
    _moba_operands(i, za_ref, km_ref, qa_ref, ka_ref, va_ref, nblk)
    _fox_operands(zb_ref, fb_ref[...], cum_ref, qb_ref, kb_ref, vb_ref)


def _inproj(x2, g, w1, bf_lane, layer):
    s = x2.shape[0]
    nblk = s // MOBA_BLOCK
    hb = pl.BlockSpec((N_HEADS, TM, LANES), lambda i: (0, i, 0))
    aug = jax.ShapeDtypeStruct((N_HEADS, s, LANES), bf16)
    return pl.pallas_call(
        functools.partial(_inproj_kernel, nblk=nblk),
        grid=(s // TM,),
        in_specs=[pl.BlockSpec((TM, D_MODEL), lambda i: (i, 0)),
                  pl.BlockSpec((1, D_MODEL), lambda i: (0, 0)),
                  _layer_weight(w1, layer),
                  pl.BlockSpec((1, LANES), lambda i: (0, 0))],
        out_specs=[hb] * 6 + [pl.BlockSpec((TM, _C_W), lambda i: (i, 0))],
        out_shape=[aug] * 6 + [jax.ShapeDtypeStruct((s, _C_W), bf16)],
        scratch_shapes=[pltpu.VMEM((nblk, N_HEADS * HEAD_DIM), f32), pltpu.VMEM((1, LANES), f32),
                        pltpu.VMEM((TM, _QKV_A[1] - _QKV_A[0]), f32),
                        pltpu.VMEM((TM, _QKV_B[1] - _QKV_B[0]), f32), pltpu.VMEM((TM, LANES), f32)],
        compiler_params=_params("arbitrary"),
        name="inproj",
    )(x2, g, w1, bf_lane)


def _flash_step(q, k_ref, v_ref, j, carry, add_bias):
    m, acc = carry
    off = pl.multiple_of(j * TA, TA)
    k = k_ref[0, pl.ds(off, TA), :]
    v = v_ref[0, pl.ds(off, TA), :]
    s = lax.dot_general(q, k, (((1,), (1,)), ((), ())), preferred_element_type=f32)
    s = add_bias(s)
    m_new = jnp.maximum(m, jnp.max(s, axis=-1, keepdims=True))
    alpha = jnp.exp2(m - m_new)
    p = jnp.exp2(s - m_new)
    acc = alpha * acc + jnp.dot(p.astype(bf16), v, preferred_element_type=f32)
    return m_new, acc


def _flash_loop(lo, hi, step, carry):
    for width in (FLASH_UNROLL, 2, 1):
        trips = (hi - lo) // width

        def body(t, c, width=width, lo=lo):
            for u in range(width):
                c = step(lo + width * t + u, c)
            return c

        carry = lax.fori_loop(0, trips, body, carry)
        lo = lo + width * trips
    return carry


def _flash_finish(o_ref, acc):
    o_ref[0] = (acc / acc[:, HEAD_DIM:HEAD_DIM + 1]).astype(o_ref.dtype)


def _flash_init():
    return jnp.full((TA, 1), -jnp.inf, f32), jnp.zeros((TA, LANES), f32)


def _fox_flash_kernel(q_ref, k_ref, v_ref, o_ref):
    i = pl.program_id(1)
    q = q_ref[0]
    carry = _flash_loop(
        0, i, lambda j, c: _flash_step(q, k_ref, v_ref, j, c, lambda s: s), _flash_init())
    row = lax.broadcasted_iota(jnp.int32, (TA, TA), 0)
    col = lax.broadcasted_iota(jnp.int32, (TA, TA), 1)
    _, acc = _flash_step(q, k_ref, v_ref, i, carry, lambda s: jnp.where(col <= row, s, NEG))
    _flash_finish(o_ref, acc)


def _moba_flash_kernel(far_ref, q_ref, k_ref, v_ref, bias_ref, o_ref, *, near):
    h = pl.program_id(0)
    i = pl.program_id(1)
    q = q_ref[0]
    far = far_ref[h]
    n_far = jnp.maximum(i - (near - 1), 0)
    carry = _flash_loop(
        0, n_far, lambda j, c: _flash_step(q, k_ref, v_ref, j, c, lambda s: s + far), _flash_init())
    carry = _flash_loop(
        n_far, i + 1,
        lambda j, c: _flash_step(q, k_ref, v_ref, j, c, lambda s: s + bias_ref[0, i - j]), carry)
    _flash_finish(o_ref, carry[1])


def _flash_specs(s):
    qspec = pl.BlockSpec((1, TA, LANES), lambda h, i: (h, i, 0))
    kvspec = pl.BlockSpec((1, s, LANES), lambda h, i: (h, 0, 0))
    return qspec, kvspec


def _fox_flash(q, k, v):
    s = q.shape[1]
    qspec, kvspec = _flash_specs(s)
    return pl.pallas_call(
        _fox_flash_kernel,
        grid=(N_HEADS, s // TA),
        in_specs=[qspec, kvspec, kvspec],
        out_specs=qspec,
        out_shape=jax.ShapeDtypeStruct((N_HEADS, s, LANES), bf16),
        compiler_params=_params("arbitrary", "arbitrary"),
        name="fox_flash",
    )(q, k, v)


def _moba_flash(far, q, k, v, bias):
    s = q.shape[1]
    near = bias.shape[1]
    qspec, kvspec = _flash_specs(s)
    return pl.pallas_call(
        functools.partial(_moba_flash_kernel, near=near),
        grid=(N_HEADS, s // TA),
        in_specs=[pl.BlockSpec(memory_space=pltpu.SMEM), qspec, kvspec, kvspec,
                  _resident((1, near, TA, TA), lambda h, i: (h, 0, 0, 0))],
        out_specs=qspec,
        out_shape=jax.ShapeDtypeStruct((N_HEADS, s, LANES), bf16),
        compiler_params=_params("arbitrary", "arbitrary"),
        name="moba_flash",
    )(far, q, k, v, bias)


def _swa_kernel(cur_ref, prev_ref, bias_ref, o_ref):
    i = pl.program_id(0)
    first = jnp.where(i == 0, 1, 0)
    lane = lax.broadcasted_iota(jnp.int32, (Q_BLOCK, LANES), 1)
    qb = Q_BLOCK
    ones = jnp.ones((2 * qb, LANES), bf16)
    for b in range(TM // qb):
        if b == 0:
            kband = jnp.concatenate([prev_ref[TM - qb:TM, _C_K:_C_V], cur_ref[0:qb, _C_K:_C_V]], axis=0)
            vband = jnp.concatenate([prev_ref[TM - qb:TM, _C_V:_C_W], cur_ref[0:qb, _C_V:_C_W]], axis=0)
            variant = first
        else:
            kband = cur_ref[qb * (b - 1):qb * (b + 1), _C_K:_C_V]
            vband = cur_ref[qb * (b - 1):qb * (b + 1), _C_V:_C_W]
            variant = 0
        kband = jnp.where(lax.broadcasted_iota(jnp.int32, kband.shape, 0) == 0, 0.0, kband).astype(bf16)
        vband = jnp.where(lax.broadcasted_iota(jnp.int32, vband.shape, 0) == 0, 0.0, vband).astype(bf16)
        outs = []
        for g in range(C_HEADS // C_GROUP):
            qs = jnp.concatenate(
                [cur_ref[qb * b:qb * (b + 1), LANES * (C_GROUP * g + u):LANES * (C_GROUP * g + u + 1)]
                 for u in range(C_GROUP)], axis=0)
            s = lax.dot_general(qs, kband, (((1,), (1,)), ((), ())), preferred_element_type=f32)
            s = s + bias_ref[variant, g]
            p = jnp.exp(s - jnp.max(s, axis=-1, keepdims=True)).astype(bf16)
            den = jnp.dot(p, ones, preferred_element_type=f32)
            for u in range(C_GROUP):
                half = (C_GROUP * g + u) % 2
                vg = vband[:, 0:LANES] if g == half else vband[:, LANES:2 * LANES]
                pv = jnp.dot(p[qb * u:qb * (u + 1)], vg, preferred_element_type=f32)
                outs.append(pv / den[qb * u:qb * (u + 1)])
        for pr in range(C_HEADS // 2):
            o_ref[qb * b:qb * (b + 1), LANES * pr:LANES * (pr + 1)] = jnp.where(
                lane < HEAD_DIM, outs[2 * pr], outs[2 * pr + 1]).astype(o_ref.dtype)


def _swa(c, bias, layer):
    s = c.shape[0]
    return pl.pallas_call(
        _swa_kernel,
        grid=(s // TM,),
        in_specs=[pl.BlockSpec((TM, _C_W), lambda i: (i, 0)),
                  pl.BlockSpec((TM, _C_W), lambda i: (jnp.maximum(i - 1, 0), 0)),
                  pl.BlockSpec((None,) + bias.shape[1:], lambda i: (layer, 0, 0, 0, 0))],
        out_specs=pl.BlockSpec((TM, 512), lambda i: (i, 0)),
        out_shape=jax.ShapeDtypeStruct((s, 512), bf16),
        compiler_params=_params("arbitrary"),
        name="swa",
    )(c, c, bias)


def _mix_ffn_kernel(x_ref, g1_ref, wg_ref, oa_ref, ob_ref, oc_ref, wpa_ref, wpb_ref, wpc_ref, wo_ref,
                    g2_ref, wup_ref, cw_ref, cb_ref, wdn_ref, gf_ref, out_ref,
                    mg_ref, halo_ref, u_ref, act_ref, *, final):
    @pl.when(pl.program_id(0) == 0)
    def _():
        halo_ref[...] = jnp.zeros_like(halo_ref)

    x = x_ref[...]
    h = _rms(x, g1_ref[...]).astype(bf16)
    oc = oc_ref[...]
    lane = lax.broadcasted_iota(jnp.int32, (TM, LANES), 1)

    def head_pairs(o_ref):
        return [jnp.where(lane < HEAD_DIM, o_ref[2 * a].astype(f32),
                          pltpu.roll(o_ref[2 * a + 1].astype(f32), HEAD_DIM, 1)).astype(bf16)
                for a in range(N_HEADS // 2)]

    oa, ob = head_pairs(oa_ref), head_pairs(ob_ref)
    cw = 512
    for n in range(D_MODEL // cw):
        lo = cw * n

        def heads(pairs, w_ref):
            acc = jnp.dot(pairs[0], w_ref[0:LANES, lo:lo + cw], preferred_element_type=f32)
            for a in range(1, N_HEADS // 2):
                acc += jnp.dot(pairs[a], w_ref[LANES * a:LANES * (a + 1), lo:lo + cw],
                               preferred_element_type=f32)
            return acc

        def gate(which):
            z = jnp.dot(h, wg_ref[:, D_MODEL * which + lo:D_MODEL * which + lo + cw],
                        preferred_element_type=f32)
            return jax.nn.sigmoid(z)

        mg = gate(0) * heads(oa, wpa_ref)
        mg += gate(1) * heads(ob, wpb_ref)
        mg += gate(2) * jnp.dot(oc, wpc_ref[:, lo:lo + cw], preferred_element_type=f32)
        mg_ref[:, lo:lo + cw] = mg.astype(bf16)
    x = x + jnp.dot(mg_ref[...], wo_ref[...], preferred_element_type=f32)

    h2 = _rms(jnp.concatenate([halo_ref[...], x], axis=0), g2_ref[...]).astype(bf16)
    halo_ref[...] = x[TM - HALO:TM]

    def conv(lo):
        u_ref[...] = jnp.dot(h2, wup_ref[:, lo:lo + FF_CHUNK], preferred_element_type=f32)
        w = cw_ref[:, lo:lo + FF_CHUNK]
        return (w[2:3] * u_ref[HALO:HALO + TM] + w[1:2] * u_ref[HALO - 1:HALO - 1 + TM]
                + w[0:1] * u_ref[HALO - 2:HALO - 2 + TM] + cb_ref[:, lo:lo + FF_CHUNK])

    for c in range(D_FF // FF_CHUNK):
        a = conv(FF_CHUNK * c)
        b = conv(D_FF + FF_CHUNK * c)
        act_ref[:, FF_CHUNK * c:FF_CHUNK * (c + 1)] = (a * jax.nn.sigmoid(a) * b).astype(bf16)
    o = x + jnp.dot(act_ref[...], wdn_ref[...], preferred_element_type=f32)
    if final:
        o = _rms(o, gf_ref[...])
    out_ref[...] = o


def _mix_ffn(x2, g1, wg, oa, ob, oc, wpa, wpb, wpc, wo, g2, wup, cw, cb, wdn, gf, layer):
    s = x2.shape[0]
    hb = pl.BlockSpec((N_HEADS, TM, LANES), lambda i: (0, i, 0))
    full = lambda a: _layer_weight(a, layer)
    vec = lambda a: pl.BlockSpec(a.shape, lambda i: (0,) * a.ndim)
    row = lambda w: pl.BlockSpec((TM, w), lambda i: (i, 0))
    return pl.pallas_call(
        functools.partial(_mix_ffn_kernel, final=(layer == DEPTH - 1)),
        grid=(s // TM,),
        in_specs=[row(D_MODEL), vec(g1), full(wg), hb, hb, row(512),
                  full(wpa), full(wpb), full(wpc), full(wo),
                  vec(g2), full(wup), vec(cw), vec(cb), full(wdn), vec(gf)],
        out_specs=row(D_MODEL),
        out_shape=jax.ShapeDtypeStruct((s, D_MODEL), f32),
        scratch_shapes=[pltpu.VMEM((TM, D_MODEL), bf16), pltpu.VMEM((HALO, D_MODEL), f32),
                        pltpu.VMEM((TM + HALO, FF_CHUNK), f32), pltpu.VMEM((TM, D_FF), bf16)],
        compiler_params=_params("arbitrary"),
        name="mix_ffn",
    )(x2, g1, wg, oa, ob, oc, wpa, wpb, wpc, wo, g2, wup, cw, cb, wdn, gf)


def _fused_in_weight(w_in):
    l = w_in.shape[0]
    wb = w_in.astype(bf16)
    cuts = np.cumsum([0, 256, 256, 256, 256, 256, 256, N_HEADS, 512, 128, 128]).tolist()
    fb, qc, kc, vc = (wb[..., cuts[n]:cuts[n + 1]] for n in range(6, 10))
    qc = (qc * SCALE).reshape(l, D_MODEL, C_HEADS, HEAD_DIM)
    zero = jnp.zeros_like(qc[:, :, 0])
    qc_pad = jnp.concatenate(
        [jnp.concatenate([qc[:, :, hq], zero] if hq < C_GROUP else [zero, qc[:, :, hq]], axis=-1)
         for hq in range(C_HEADS)], axis=-1)
    vc2 = jnp.concatenate([vc, vc[..., HEAD_DIM:], vc[..., :HEAD_DIM]], axis=-1)
    fbp = jnp.pad(fb, ((0, 0), (0, 0), (0, LANES - N_HEADS)))
    w1 = jnp.concatenate([wb[..., :cuts[6]], qc_pad, kc, vc2, fbp], axis=-1)
    assert w1.shape[-1] == _W1_COLS and w1.dtype == bf16
    return w1, wb[..., cuts[10]:]


def _bias_lookup(tbl, dist):
    bucket = _t5_bucket(jnp.asarray(dist, jnp.int32))
    out = jnp.zeros((tbl.shape[0],) + bucket.shape, f32)
    for b in range(NUM_BUCKETS):
        col = tbl[:, b].astype(f32).reshape((-1,) + (1,) * bucket.ndim)
        out = jnp.where(bucket[None] == b, col, out)
    return out


def _toeplitz_kernel(u_ref, o_ref):
    x = jnp.broadcast_to(u_ref[0, 0], (TA, 2 * TA))
    o_ref[0, 0] = pltpu.roll(x, 0, 1, stride=1, stride_axis=0)[:, :TA]


def _moba_bias_tiles(tbl_a, s):
    near = min(-(-(MAX_DISTANCE + TA - 1) // TA), s // TA)
    m = np.arange(2 * TA)
    rel = np.where(m < TA, -m, np.where(m == TA, 0, 2 * TA - m))
    dist = np.arange(near)[:, None] * TA + rel[None, :]
    gen = jnp.where(dist >= 0, _bias_lookup(tbl_a, np.maximum(dist, 0)) * LOG2E, NEG)
    tiles = pl.pallas_call(
        _toeplitz_kernel,
        grid=(N_HEADS, near),
        in_specs=[pl.BlockSpec((1, 1, 1, 2 * TA), lambda h, d: (h, d, 0, 0))],
        out_specs=pl.BlockSpec((1, 1, TA, TA), lambda h, d: (h, d, 0, 0)),
        out_shape=jax.ShapeDtypeStruct((N_HEADS, near, TA, TA), f32),
        compiler_params=_params("arbitrary", "arbitrary"),
        name="moba_bias_tiles",
    )(gen[:, :, None, :])
    far = _bias_lookup(tbl_a, np.asarray([near * TA]))[:, 0] * LOG2E
    return tiles, far


def _swa_tables(tbl_c, sinks):
    tl = np.arange(Q_BLOCK)
    sl = np.arange(2 * Q_BLOCK)
    dist = tl[:, None] + Q_BLOCK - sl[None, :]
    bias = _bias_lookup(tbl_c, dist)
    band = (dist >= 0) & (dist < WINDOW)
    assert not band[:, 0].any()
    masks = np.stack([band, band & (sl[None, :] >= Q_BLOCK)])
    table = jnp.where(masks[:, None], bias[None], NEG)
    sink = sinks.astype(f32)[:, None, :, None, None]
    table = jnp.where((sl == 0)[None, None, None, None, :], sink, table[None])
    return table.reshape(sinks.shape[0], 2, C_HEADS // C_GROUP, C_GROUP * Q_BLOCK, 2 * Q_BLOCK)


def kernel(x, ln1, w_in, b_f, sinks, w_pa, w_pb, w_pc, w_o, ln2, w_up, conv_w, conv_b, w_down,
           rel_bias, ln_f):
    bsz, s, _ = x.shape
    assert bsz == 1 and s % TM == 0 and s % TA == 0 and TM % MOBA_BLOCK == 0
    assert s // MOBA_BLOCK <= HEAD_DIM
    w1, wg = _fused_in_weight(w_in)
    wpa, wpb, wpc, wo, wup, wdn = (w.astype(bf16) for w in (w_pa, w_pb, w_pc, w_o, w_up, w_down))
    tbl_a = rel_bias[:, :N_HEADS].T
    tbl_c = rel_bias[:, N_HEADS:].T
    moba_tiles, moba_far = _moba_bias_tiles(tbl_a, s)
    swa_bias = _swa_tables(tbl_c, sinks)
    bf_lane = jnp.pad(b_f.astype(f32), ((0, 0), (0, LANES - N_HEADS)))[:, None, :]
    gf = ln_f.astype(f32)[None, :]

    x2 = x[0]
    for l in range(DEPTH):
        qa, ka, va, qb, kb, vb, c = _inproj(x2, ln1[l][None, :], w1, bf_lane[l], l)
        oa = _moba_flash(moba_far, qa, ka, va, moba_tiles)
        ob = _fox_flash(qb, kb, vb)
        oc = _swa(c, swa_bias, l)
        x2 = _mix_ffn(x2, ln1[l][None, :], wg, oa, ob, oc, wpa, wpb, wpc, wo,
                      ln2[l][None, :], wup, conv_w[l], conv_b[l][None, :], wdn, gf, l)
    return x2[None]
```
